```python
import math
import jax, jax.numpy as jnp
from jax import lax
import numpy as np

D_MODEL = 1024
BATCH = 4
SEQ = 8192
DEPTH = 2

N_MIXERS = 2
ATTN_HEADS = 16
ATTN_KV_HEADS = 2
ATTN_HEAD_DIM = 64
WINDOW = 128
BLOCK = 128
ATTN_QKV_WIDTH = (ATTN_HEADS + 2 * ATTN_KV_HEADS) * ATTN_HEAD_DIM
MLSTM_HEADS = 8
MLSTM_QK_DIM = 64
MLSTM_V_DIM = 128
CHUNK = 64
MLSTM_IN_WIDTH = MLSTM_HEADS * (2 * MLSTM_QK_DIM + 2 * MLSTM_V_DIM) + 2 * MLSTM_HEADS
D_FF = 2816
CONV_WIDTH = 3
PLE_DIM = 256
LN_EPS = 1e-5
ALPHA = (2.0 * DEPTH) ** 0.25
BETA = (8.0 * DEPTH) ** -0.25

kernel_name = "hybrid_swa_sink_mlstm_convffn_deepnorm"


def _n_layers_of(r):
    return (DEPTH - r + N_MIXERS - 1) // N_MIXERS


N_ATTN_LAYERS = _n_layers_of(0)
N_MLSTM_LAYERS = _n_layers_of(1)


def layer_norm(x, g, b):
    xf = x.astype(jnp.float32)
    mu = jnp.mean(xf, axis=-1, keepdims=True)
    var = jnp.mean(jnp.square(xf - mu), axis=-1, keepdims=True)
    y = (xf - mu) * lax.rsqrt(var + LN_EPS)
    return (y * g.astype(jnp.float32) + b.astype(jnp.float32)).astype(x.dtype)


def alibi_slopes(n_heads):
    s = 2.0 ** (-8.0 * np.arange(1, n_heads + 1) / n_heads)
    return jnp.asarray(s, dtype=jnp.float32)


def sliding_window_attention(x, w_qkv, b_qkv, sinks, w_o):
    B, S, _ = x.shape
    H, KVH, HD = ATTN_HEADS, ATTN_KV_HEADS, ATTN_HEAD_DIM
    G = H // KVH
    nb = S // BLOCK
    qkv = x @ w_qkv + b_qkv
    q, k, v = jnp.split(qkv, [H * HD, (H + KVH) * HD], axis=-1)
    q = q.reshape(B, nb, BLOCK, KVH, G, HD)
    k = k.reshape(B, nb, BLOCK, KVH, HD)
    v = v.reshape(B, nb, BLOCK, KVH, HD)

    def with_prev(t):
        prev = jnp.concatenate([jnp.zeros_like(t[:, :1]), t[:, :-1]], axis=1)
        return jnp.concatenate([prev, t], axis=2)

    kb, vb = with_prev(k), with_prev(v)
    scores = jnp.einsum('bnqkgd,bnskd->bkgnqs', q, kb).astype(jnp.float32) * (HD ** -0.5)
    qi = jnp.arange(BLOCK)[:, None]
    sj = jnp.arange(2 * BLOCK)[None, :]
    dist = qi + BLOCK - sj
    blk = jnp.arange(nb)[:, None, None]
    key_pos = (blk - 1) * BLOCK + sj[None]
    valid = (dist >= 0)[None] & (dist < WINDOW)[None] & (key_pos >= 0)
    slopes = alibi_slopes(H).reshape(KVH, G)[None, :, :, None, None, None]
    logits = scores - slopes * dist.astype(jnp.float32)
    logits = jnp.where(valid, logits, -jnp.inf)
    sink = sinks.astype(jnp.float32).reshape(KVH, G)[None, :, :, None, None, None]
    m = jnp.maximum(jnp.max(logits, axis=-1, keepdims=True), sink)
    pexp = jnp.exp(logits - m)
    denom = jnp.sum(pexp, axis=-1, keepdims=True) + jnp.exp(sink - m)
    probs = (pexp / denom).astype(vb.dtype)
    out = jnp.einsum('bkgnqs,bnskd->bnqkgd', probs, vb).reshape(B, S, H * HD)
    return out @ w_o


def mlstm(x, w_in, b_gates, w_out):
    B, S, _ = x.shape
    H, DK, DV, L = MLSTM_HEADS, MLSTM_QK_DIM, MLSTM_V_DIM, CHUNK
    nc = S // L
    proj = x @ w_in
    q, k, v, o, gates = jnp.split(
        proj, [H * DK, 2 * H * DK, 2 * H * DK + H * DV, 2 * H * DK + 2 * H * DV], axis=-1)
    gates = (gates + b_gates).astype(jnp.float32)
    i_pre = gates[..., :H]
    log_f = jax.nn.log_sigmoid(gates[..., H:])

    def to_chunks(t, d):
        return t.reshape(B, nc, L, H, d).transpose(1, 0, 3, 2, 4).astype(jnp.float32)

    def gate_chunks(g):
        return g.reshape(B, nc, L, H).transpose(1, 0, 3, 2)

    qc = to_chunks(q, DK)
    kc = to_chunks(k, DK) * (DK ** -0.5)
    vc = to_chunks(v, DV)
    ic = gate_chunks(i_pre)
    fc = gate_chunks(log_f)
    causal = jnp.tril(jnp.ones((L, L), dtype=bool))

    def step(carry, xs):
        C, n, m = carry
        q_, k_, v_, i_, lf = xs
        b = jnp.cumsum(lf, axis=-1)
        log_inter = b + m[..., None]
        log_intra = b[..., :, None] - b[..., None, :] + i_[..., None, :]
        log_intra = jnp.where(causal, log_intra, -jnp.inf)
        m_t = jnp.maximum(log_inter, jnp.max(log_intra, axis=-1))
        w_inter = jnp.exp(log_inter - m_t)
        w_intra = jnp.exp(log_intra - m_t[..., None])
        qk = jnp.einsum('bhtd,bhsd->bhts', q_, k_) * w_intra
        num = (w_inter[..., None] * jnp.einsum('bhtd,bhde->bhte', q_, C)
               + jnp.einsum('bhts,bhse->bhte', qk, v_))
        den = w_inter * jnp.einsum('bhtd,bhd->bht', q_, n) + jnp.sum(qk, axis=-1)
        h = num / jnp.maximum(jnp.abs(den), jnp.exp(-m_t))[..., None]
        b_L = b[..., -1]
        log_state = b_L[..., None] - b + i_
        m_new = jnp.maximum(b_L + m, jnp.max(log_state, axis=-1))
        decay = jnp.exp(b_L + m - m_new)
        w_s = jnp.exp(log_state - m_new[..., None])
        kw = k_ * w_s[..., None]
        C_new = decay[..., None, None] * C + jnp.einsum('bhsd,bhse->bhde', kw, v_)
        n_new = decay[..., None] * n + jnp.sum(kw, axis=2)
        return (C_new, n_new, m_new), h

    init = (jnp.zeros((B, H, DK, DV), jnp.float32),
            jnp.zeros((B, H, DK), jnp.float32),
            jnp.zeros((B, H), jnp.float32))
    _, hs = lax.scan(step, init, (qc, kc, vc, ic, fc))
    h = hs.transpose(1, 0, 3, 2, 4).reshape(B, S, H * DV).astype(x.dtype)
    h = jax.nn.sigmoid(o) * h
    return h @ w_out


def causal_depthwise_conv(u, w, b):
    y = lax.conv_general_dilated(
        u, w[:, None, :].astype(u.dtype), window_strides=(1,),
        padding=[(CONV_WIDTH - 1, 0)], dimension_numbers=('NWC', 'WIO', 'NWC'),
        feature_group_count=u.shape[-1])
    return y + b


def conv_ffn(x, w_up, conv_w, conv_b, w_down):
    h = x @ w_up
    h = causal_depthwise_conv(h, conv_w, conv_b)
    g, u = jnp.split(h, 2, axis=-1)
    return (jax.nn.gelu(g, approximate=True) * u) @ w_down


def setup_inputs(seed: int = 0) -> dict:
    key = jax.random.key(seed)
    ks = jax.random.split(key, 24)
    nrm = lambda k, shape: jax.random.normal(k, shape, jnp.float32)
    NA, NM = N_ATTN_LAYERS, N_MLSTM_LAYERS
    f_bias = jnp.broadcast_to(jnp.linspace(3.0, 6.0, MLSTM_HEADS, dtype=jnp.float32), (NM, MLSTM_HEADS))
    mlstm_b_gates = jnp.concatenate(
        [0.1 * nrm(ks[8], (NM, MLSTM_HEADS)), f_bias + 0.1 * nrm(ks[9], (NM, MLSTM_HEADS))], axis=-1)
    return {
        "x": nrm(ks[0], (BATCH, SEQ, D_MODEL)),
        "p": nrm(ks[1], (DEPTH, BATCH, SEQ, PLE_DIM)),
        "attn_w_qkv": nrm(ks[2], (NA, D_MODEL, ATTN_QKV_WIDTH)) * D_MODEL ** -0.5,
        "attn_b_qkv": 0.02 * nrm(ks[3], (NA, ATTN_QKV_WIDTH)),
        "attn_sinks": 0.5 * nrm(ks[4], (NA, ATTN_HEADS)),
        "attn_w_o": nrm(ks[5], (NA, ATTN_HEADS * ATTN_HEAD_DIM, D_MODEL)) * (ATTN_HEADS * ATTN_HEAD_DIM) ** -0.5 * BETA,
        "mlstm_w_in": nrm(ks[6], (NM, D_MODEL, MLSTM_IN_WIDTH)) * D_MODEL ** -0.5,
        "mlstm_b_gates": mlstm_b_gates,
        "mlstm_w_out": nrm(ks[7], (NM, MLSTM_HEADS * MLSTM_V_DIM, D_MODEL)) * (MLSTM_HEADS * MLSTM_V_DIM) ** -0.5 * BETA,
        "ln1_g": 1.0 + 0.02 * nrm(ks[10], (DEPTH, D_MODEL)),
        "ln1_b": 0.02 * nrm(ks[11], (DEPTH, D_MODEL)),
        "ffn_w_up": nrm(ks[12], (DEPTH, D_MODEL, 2 * D_FF)) * D_MODEL ** -0.5,
        "ffn_conv_w": nrm(ks[13], (DEPTH, CONV_WIDTH, 2 * D_FF)) * CONV_WIDTH ** -0.5,
        "ffn_conv_b": 0.02 * nrm(ks[14], (DEPTH, 2 * D_FF)),
        "ffn_w_down": nrm(ks[15], (DEPTH, D_FF, D_MODEL)) * D_FF ** -0.5 * BETA,
        "ple_w": nrm(ks[16], (DEPTH, PLE_DIM, D_MODEL)) * PLE_DIM ** -0.5,
        "ple_gate_w": nrm(ks[17], (DEPTH, D_MODEL, D_MODEL)) * D_MODEL ** -0.5,
        "ple_gate_b": 0.02 * nrm(ks[18], (DEPTH, D_MODEL)),
        "ln2_g": 1.0 + 0.02 * nrm(ks[19], (DEPTH, D_MODEL)),
        "ln2_b": 0.02 * nrm(ks[20], (DEPTH, D_MODEL)),
    }


def reference(x, p, attn_w_qkv, attn_b_qkv, attn_sinks, attn_w_o,
              mlstm_w_in, mlstm_b_gates, mlstm_w_out,
              ln1_g, ln1_b, ffn_w_up, ffn_conv_w, ffn_conv_b, ffn_w_down,
              ple_w, ple_gate_w, ple_gate_b, ln2_g, ln2_b):
    for i in range(DEPTH):
        j = i // N_MIXERS
        if i % N_MIXERS == 0:
            mix = sliding_window_attention(x, attn_w_qkv[j], attn_b_qkv[j], attn_sinks[j], attn_w_o[j])
        else:
            mix = mlstm(x, mlstm_w_in[j], mlstm_b_gates[j], mlstm_w_out[j])
        x = layer_norm(ALPHA * x + mix, ln1_g[i], ln1_b[i])
        ple = (p[i] @ ple_w[i]) * jax.nn.sigmoid(x @ ple_gate_w[i] + ple_gate_b[i])
        ffn = conv_ffn(x, ffn_w_up[i], ffn_conv_w[i], ffn_conv_b[i], ffn_w_down[i])
        x = layer_norm(ALPHA * x + ffn + ple, ln2_g[i], ln2_b[i])
    return x
```

```python
import functools

import numpy as np
import jax
import jax.numpy as jnp
from jax import lax
from jax.experimental import pallas as pl
from jax.experimental.pallas import tpu as pltpu

D_MODEL = 1024
DEPTH = 2
N_MIXERS = 2
ATTN_HEADS = 16
ATTN_KV_HEADS = 2
ATTN_HEAD_DIM = 64
WINDOW = 128
BLOCK = 128
MLSTM_HEADS = 8
MLSTM_QK_DIM = 64
MLSTM_V_DIM = 128
D_FF = 2816
CONV_WIDTH = 3
PLE_DIM = 256
LN_EPS = 1e-5
ALPHA = (2.0 * DEPTH) ** 0.25

LANES = 128
MLSTM_CHUNK = 128
FF_CHUNK = 256
N_FF_CHUNKS = D_FF // FF_CHUNK
CONV_HALO = 8
SEQ_TILE = 512
VMEM_LIMIT = 52 * 1024 * 1024

F32 = jnp.float32
BF16 = jnp.bfloat16
NEG_INF = float("-inf")


def _layer_norm(y, g, b):
    mu = jnp.mean(y, axis=-1, keepdims=True)
    d = y - mu
    var = jnp.mean(d * d, axis=-1, keepdims=True)
    return d * lax.rsqrt(var + LN_EPS) * g + b


def _const_spec(shape):
    nd = len(shape)
    return pl.BlockSpec(shape, lambda b, s: (0,) * nd, pipeline_mode=pl.Buffered(1))


def _tile_spec(tile, width):
    return pl.BlockSpec((None, tile, width), lambda b, s: (b, s, 0))


def _attn_kernel(x_ref, wqkv_ref, bqkv_ref, sinks_ref, bias_ref, wo_ref, g_ref, b_ref, o_ref,
                 q_s, k_s, v_s, att_s):
    j = pl.program_id(1)
    tq = x_ref.shape[0]
    nblk = tq // BLOCK
    hq = ATTN_HEADS * ATTN_HEAD_DIM
    hkv = ATTN_KV_HEADS * ATTN_HEAD_DIM

    @pl.when(j == 0)
    def _():
        k_s[:, 0:BLOCK, :] = jnp.zeros((4, BLOCK, LANES), BF16)
        v_s[:, 0:BLOCK, :] = jnp.zeros((4, BLOCK, LANES), BF16)

    x = x_ref[...]
    qkv = jnp.dot(x.astype(BF16), wqkv_ref[...], preferred_element_type=F32) + bqkv_ref[...]
    q_s[...] = (qkv[:, :hq] * (ATTN_HEAD_DIM ** -0.5)).astype(BF16)
    lo = lax.broadcasted_iota(jnp.int32, (tq, LANES), 1) < ATTN_HEAD_DIM
    for src, dst in ((qkv[:, hq:hq + hkv], k_s), (qkv[:, hq + hkv:hq + 2 * hkv], v_s)):
        rolled = pltpu.roll(src, ATTN_HEAD_DIM, axis=1)
        dst[0, BLOCK:, :] = jnp.where(lo, src, 0.0).astype(BF16)
        dst[1, BLOCK:, :] = jnp.where(lo, 0.0, rolled).astype(BF16)
        dst[2, BLOCK:, :] = jnp.where(lo, rolled, 0.0).astype(BF16)
        dst[3, BLOCK:, :] = jnp.where(lo, 0.0, src).astype(BF16)

    def blk_body(bi, carry):
        r0 = pl.multiple_of(bi * BLOCK, BLOCK)
        first = jnp.logical_and(j == 0, bi == 0).astype(jnp.int32)
        for t in range(ATTN_HEADS // 2):
            kvh = t // (ATTN_HEADS // ATTN_KV_HEADS // 2)
            qt = q_s[pl.ds(r0, BLOCK), t * LANES:(t + 1) * LANES]
            acc = None
            for par in range(2):
                h = 2 * t + par
                kb = k_s[2 * kvh + par, pl.ds(r0, 2 * BLOCK), :]
                vb = v_s[2 * kvh + par, pl.ds(r0, 2 * BLOCK), :]
                s = lax.dot_general(qt, kb, (((1,), (1,)), ((), ())), preferred_element_type=F32)
                logits = s + bias_ref[first, h]
                sink = sinks_ref[h]
                m = jnp.maximum(jnp.max(logits, axis=-1, keepdims=True), sink)
                pexp = jnp.exp(logits - m)
                denom = jnp.sum(pexp, axis=-1, keepdims=True) + jnp.exp(sink - m)
                o = jnp.dot(pexp.astype(BF16), vb, preferred_element_type=F32) * (1.0 / denom)
                acc = o if acc is None else acc + o
            att_s[pl.ds(r0, BLOCK), t * LANES:(t + 1) * LANES] = acc.astype(BF16)
        return carry

    lax.fori_loop(0, nblk, blk_body, 0)
    k_s[:, 0:BLOCK, :] = k_s[:, tq:tq + BLOCK, :]
    v_s[:, 0:BLOCK, :] = v_s[:, tq:tq + BLOCK, :]

    mix = jnp.dot(att_s[...], wo_ref[...], preferred_element_type=F32)
    o_ref[...] = _layer_norm(ALPHA * x + mix, g_ref[...], b_ref[...])


def _attn_bias():
    qi = np.arange(BLOCK)[:, None]
    sj = np.arange(2 * BLOCK)[None, :]
    dist = qi + BLOCK - sj
    band = (dist >= 0) & (dist < WINDOW)
    slopes = 2.0 ** (-8.0 * np.arange(1, ATTN_HEADS + 1) / ATTN_HEADS)
    base = -slopes[:, None, None] * dist[None].astype(np.float64)
    general = np.where(band[None], base, -np.inf)
    first = np.where((band & (sj >= BLOCK))[None], base, -np.inf)
    return jnp.asarray(np.stack([general, first]), dtype=F32)


def _attn_layer(x, w_qkv, b_qkv, sinks, w_o, ln_g, ln_b, tile):
    B, S, D = x.shape
    hq = ATTN_HEADS * ATTN_HEAD_DIM
    qkv_w = w_qkv.shape[1]
    bias = _attn_bias()
    return pl.pallas_call(
        _attn_kernel,
        name="attn_layer",
        grid=(B, S // tile),
        in_specs=[
            _tile_spec(tile, D),
            _const_spec((D, qkv_w)),
            _const_spec((1, qkv_w)),
            pl.BlockSpec(memory_space=pltpu.SMEM),
            _const_spec(bias.shape),
            _const_spec((hq, D)),
            _const_spec((1, D)),
            _const_spec((1, D)),
        ],
        out_specs=_tile_spec(tile, D),
        out_shape=jax.ShapeDtypeStruct((B, S, D), F32),
        scratch_shapes=[
            pltpu.VMEM((tile, hq), BF16),
            pltpu.VMEM((4, BLOCK + tile, LANES), BF16),
            pltpu.VMEM((4, BLOCK + tile, LANES), BF16),
            pltpu.VMEM((tile, hq), BF16),
        ],
        compiler_params=pltpu.CompilerParams(
            dimension_semantics=("arbitrary", "arbitrary"), vmem_limit_bytes=VMEM_LIMIT),
    )(x, w_qkv.astype(BF16), b_qkv.reshape(1, qkv_w), sinks.astype(F32), bias,
      w_o.astype(BF16), ln_g.reshape(1, D), ln_b.reshape(1, D))


def _mlstm_kernel(x_ref, win_ref, wg_ref, bg_ref, wout_ref, g_ref, b_ref, o_ref,
                  proj_s, gate_s, hg_s, c_s, n_s, m_s):
    j = pl.program_id(1)
    ts = x_ref.shape[0]
    L = MLSTM_CHUNK
    H, DK, DV = MLSTM_HEADS, MLSTM_QK_DIM, MLSTM_V_DIM
    npair = H // 2
    k_off, v_off, o_off = H * DK, 2 * H * DK, 2 * H * DK + H * DV

    @pl.when(j == 0)
    def _():
        c_s[...] = jnp.zeros(c_s.shape, F32)
        n_s[...] = jnp.zeros(n_s.shape, F32)
        m_s[...] = jnp.zeros(m_s.shape, F32)

    x = x_ref[...]
    xb = x.astype(BF16)
    proj_s[...] = jnp.dot(xb, win_ref[...], preferred_element_type=F32)
    gate_s[...] = jnp.dot(xb, wg_ref[...], preferred_element_type=F32) + bg_ref[...]

    row = lax.broadcasted_iota(jnp.int32, (L, L), 0)
    col = lax.broadcasted_iota(jnp.int32, (L, L), 1)
    causal = col <= row
    tri = causal.astype(F32)
    lo = col < DK
    bd_mask = (lax.broadcasted_iota(jnp.int32, (2 * DK, 2 * DV), 0) < DK) == (
        lax.broadcasted_iota(jnp.int32, (2 * DK, 2 * DV), 1) < DV)
    lo_c = lax.broadcasted_iota(jnp.int32, (1, 2 * DV), 1) < DV
    lo_n = lax.broadcasted_iota(jnp.int32, (1, 2 * DK), 1) < DK

    def chunk_body(ci, carry):
        r0 = pl.multiple_of(ci * L, L)
        rows = pl.ds(r0, L)
        gates = gate_s[rows, :]
        x_if = jnp.where(col < H, gates, jax.nn.log_sigmoid(gates))
        csum = jnp.dot(tri, x_if, preferred_element_type=F32, precision=lax.Precision.HIGHEST)
        y = jnp.where(col < H, gates, csum)
        yt = y.T
        for t in range(npair):
            qt = proj_s[rows, t * LANES:(t + 1) * LANES]
            kt = proj_s[rows, k_off + t * LANES:k_off + (t + 1) * LANES] * (DK ** -0.5)
            vp = proj_s[rows, v_off + 2 * t * DV:v_off + 2 * (t + 1) * DV]
            qtb = qt.astype(BF16)
            vpb = vp.astype(BF16)
            n_row = n_s[t:t + 1, :]
            q_c = jnp.dot(qtb, c_s[t].astype(BF16), preferred_element_type=F32)
            qn = qt * n_row
            w_s, decay = [], []
            for par in range(2):
                h = 2 * t + par
                half = lo if par == 0 else jnp.logical_not(lo)
                b_col, i_col = y[:, H + h:H + h + 1], y[:, h:h + 1]
                b_row, i_row = yt[H + h:H + h + 1, :], yt[h:h + 1, :]
                m_prev = m_s[h:h + 1, 0:1]
                log_inter = b_col + m_prev
                log_intra = jnp.where(causal, b_col - b_row + i_row, NEG_INF)
                m_t = jnp.maximum(log_inter, jnp.max(log_intra, axis=-1, keepdims=True))
                w_inter = jnp.exp(log_inter - m_t)
                w_intra = jnp.exp(log_intra - m_t)
                k_h = jnp.where(half, kt, 0.0).astype(BF16)
                qk = lax.dot_general(qtb, k_h, (((1,), (1,)), ((), ())),
                                     preferred_element_type=F32) * w_intra
                v_h = vpb[:, par * DV:(par + 1) * DV]
                num = w_inter * q_c[:, par * DV:(par + 1) * DV] + jnp.dot(
                    qk.astype(BF16), v_h, preferred_element_type=F32)
                den = (w_inter * jnp.sum(jnp.where(half, qn, 0.0), axis=-1, keepdims=True)
                       + jnp.sum(qk, axis=-1, keepdims=True))
                hh = num / jnp.maximum(jnp.abs(den), jnp.exp(-m_t))
                o_h = proj_s[rows, o_off + h * DV:o_off + (h + 1) * DV]
                hg_s[rows, h * DV:(h + 1) * DV] = (jax.nn.sigmoid(o_h) * hh).astype(BF16)
                b_last = b_col[L - 1:L, :]
                log_state = b_last - b_col + i_col
                m_new = jnp.maximum(b_last + m_prev, jnp.max(log_state, axis=0, keepdims=True))
                decay.append(jnp.exp(b_last + m_prev - m_new))
                w_s.append(jnp.exp(log_state - m_new))
                m_s[h:h + 1, :] = jnp.broadcast_to(m_new, (1, LANES))
            kw = kt * jnp.where(lo, w_s[0], w_s[1])
            upd = jnp.dot(kw.T.astype(BF16), vpb, preferred_element_type=F32)
            c_s[t] = jnp.where(lo_c, decay[0], decay[1]) * c_s[t] + jnp.where(bd_mask, upd, 0.0)
            n_s[t:t + 1, :] = (jnp.where(lo_n, decay[0], decay[1]) * n_row
                               + jnp.sum(kw, axis=0, keepdims=True))
        return carry

    lax.fori_loop(0, ts // L, chunk_body, 0)

    mix = jnp.dot(hg_s[...], wout_ref[...], preferred_element_type=F32)
    o_ref[...] = _layer_norm(ALPHA * x + mix, g_ref[...], b_ref[...])


def _mlstm_layer(x, w_in, b_gates, w_out, ln_g, ln_b, tile):
    B, S, D = x.shape
    H, DK, DV = MLSTM_HEADS, MLSTM_QK_DIM, MLSTM_V_DIM
    main_w = 2 * H * DK + 2 * H * DV
    w_main = w_in[:, :main_w].astype(BF16)
    w_g = jnp.pad(w_in[:, main_w:], ((0, 0), (0, LANES - 2 * H))).astype(BF16)
    b_g = jnp.pad(b_gates, (0, LANES - 2 * H)).reshape(1, LANES)
    return pl.pallas_call(
        _mlstm_kernel,
        name="mlstm_layer",
        grid=(B, S // tile),
        in_specs=[
            _tile_spec(tile, D),
            _const_spec((D, main_w)),
            _const_spec((D, LANES)),
            _const_spec((1, LANES)),
            _const_spec((H * DV, D)),
            _const_spec((1, D)),
            _const_spec((1, D)),
        ],
        out_specs=_tile_spec(tile, D),
        out_shape=jax.ShapeDtypeStruct((B, S, D), F32),
        scratch_shapes=[
            pltpu.VMEM((tile, main_w), F32),
            pltpu.VMEM((tile, LANES), F32),
            pltpu.VMEM((tile, H * DV), BF16),
            pltpu.VMEM((H // 2, 2 * DK, 2 * DV), F32),
            pltpu.VMEM((8, 2 * DK), F32),
            pltpu.VMEM((H, LANES), F32),
        ],
        compiler_params=pltpu.CompilerParams(
            dimension_semantics=("arbitrary", "arbitrary"), vmem_limit_bytes=VMEM_LIMIT),
    )(x, w_main, w_g, b_g, w_out.astype(BF16), ln_g.reshape(1, D), ln_b.reshape(1, D))


def _ffn_kernel(x_ref, p_ref, wup_ref, cw_ref, wdown_ref, wple_ref, wgate_ref, bgate_ref,
                g_ref, b_ref, o_ref, h_s, halo_s, a_s):
    j = pl.program_id(1)
    tm = x_ref.shape[0]
    fc = FF_CHUNK

    @pl.when(j == 0)
    def _():
        halo_s[...] = jnp.zeros(halo_s.shape, F32)

    x = x_ref[...]
    xb = x.astype(BF16)

    def conv_chunk(slot, col0):
        cols = slice(col0, col0 + fc)
        h = jnp.dot(xb, wup_ref[:, cols], preferred_element_type=F32)
        h_s[slot, 0:CONV_HALO, :] = halo_s[:, cols]
        h_s[slot, CONV_HALO:, :] = h
        halo_s[:, cols] = h[tm - CONV_HALO:, :]
        return (cw_ref[0:1, cols] * h_s[slot, CONV_HALO - 2:CONV_HALO - 2 + tm, :]
                + cw_ref[1:2, cols] * h_s[slot, CONV_HALO - 1:CONV_HALO - 1 + tm, :]
                + cw_ref[2:3, cols] * h + cw_ref[3:4, cols])

    for c in range(N_FF_CHUNKS):
        gate = conv_chunk(2 * (c % 2), c * fc)
        value = conv_chunk(2 * (c % 2) + 1, D_FF + c * fc)
        a_s[:, c * fc:(c + 1) * fc] = (jax.nn.gelu(gate, approximate=True) * value).astype(BF16)

    ffn = jnp.dot(a_s[...], wdown_ref[...], preferred_element_type=F32)
    pgate = jax.nn.sigmoid(jnp.dot(xb, wgate_ref[...], preferred_element_type=F32) + bgate_ref[...])
    ple = jnp.dot(p_ref[...].astype(BF16), wple_ref[...], preferred_element_type=F32) * pgate
    o_ref[...] = _layer_norm(ALPHA * x + ffn + ple, g_ref[...], b_ref[...])


def _ffn_layer(x, p, layer, w_up, conv_w, conv_b, w_down, ple_w, gate_w, gate_b, ln_g, ln_b, tile):
    B, S, D = x.shape
    cw = jnp.concatenate([conv_w, conv_b[None], jnp.zeros((8 - CONV_WIDTH - 1, 2 * D_FF), F32)], axis=0)
    return pl.pallas_call(
        _ffn_kernel,
        name="ffn_layer",
        grid=(B, S // tile),
        in_specs=[
            _tile_spec(tile, D),
            pl.BlockSpec((None, None, tile, PLE_DIM), lambda b, s: (layer, b, s, 0)),
            _const_spec((D, 2 * D_FF)),
            _const_spec((8, 2 * D_FF)),
            _const_spec((D_FF, D)),
            _const_spec((PLE_DIM, D)),
            _const_spec((D, D)),
            _const_spec((1, D)),
            _const_spec((1, D)),
            _const_spec((1, D)),
        ],
        out_specs=_tile_spec(tile, D),
        out_shape=jax.ShapeDtypeStruct((B, S, D), F32),
        scratch_shapes=[
            pltpu.VMEM((4, CONV_HALO + tile, FF_CHUNK), F32),
            pltpu.VMEM((CONV_HALO, 2 * D_FF), F32),
            pltpu.VMEM((tile, D_FF), BF16),
        ],
        compiler_params=pltpu.CompilerParams(
            dimension_semantics=("arbitrary", "arbitrary"), vmem_limit_bytes=VMEM_LIMIT),
    )(x, p, w_up.astype(BF16), cw, w_down.astype(BF16), ple_w.astype(BF16), gate_w.astype(BF16),
      gate_b.reshape(1, D), ln_g.reshape(1, D), ln_b.reshape(1, D))


def kernel(x, p, attn_w_qkv, attn_b_qkv, attn_sinks, attn_w_o, mlstm_w_in, mlstm_b_gates, mlstm_w_out, ln1_g, ln1_b, ffn_w_up, ffn_conv_w, ffn_conv_b, ffn_w_down, ple_w, ple_gate_w, ple_gate_b, ln2_g, ln2_b):
    tile = min(SEQ_TILE, x.shape[1])
    for i in range(DEPTH):
        j = i // N_MIXERS
        if i % N_MIXERS == 0:
            x = _attn_layer(x, attn_w_qkv[j], attn_b_qkv[j], attn_sinks[j], attn_w_o[j],
                            ln1_g[i], ln1_b[i], tile)
        else:
            x = _mlstm_layer(x, mlstm_w_in[j], mlstm_b_gates[j], mlstm_w_out[j],
                             ln1_g[i], ln1_b[i], tile)
        x = _ffn_layer(x, p, i, ffn_w_up[i], ffn_conv_w[i], ffn_conv_b[i], ffn_w_down[i],
                       ple_w[i], ple_gate_w[i], ple_gate_b[i], ln2_g[i], ln2_b[i], tile)
    return x
```

```python
import functools

import numpy as np
import jax
import jax.numpy as jnp
from jax import lax
from jax.experimental import pallas as pl
from jax.experimental.pallas import tpu as pltpu

D_MODEL = 1024
DEPTH = 2
N_MIXERS = 2
ATTN_HEADS = 16
ATTN_KV_HEADS = 2
ATTN_HEAD_DIM = 64
WINDOW = 128
BLOCK = 128
MLSTM_HEADS = 8
MLSTM_QK_DIM = 64
MLSTM_V_DIM = 128
D_FF = 2816
CONV_WIDTH = 3
PLE_DIM = 256
LN_EPS = 1e-5
ALPHA = (2.0 * DEPTH) ** 0.25

LANES = 128
MLSTM_CHUNK = 128
FF_CHUNK = 256
N_FF_CHUNKS = D_FF // FF_CHUNK
CONV_HALO = 8
SEQ_TILE = 512
VMEM_LIMIT = 52 * 1024 * 1024

F32 = jnp.float32
BF16 = jnp.bfloat16
NEG_INF = float("-inf")


def _layer_norm(y, g, b):
    mu = jnp.mean(y, axis=-1, keepdims=True)
    d = y - mu
    var = jnp.mean(d * d, axis=-1, keepdims=True)
    return d * lax.rsqrt(var + LN_EPS) * g + b


def _const_spec(shape):
    nd = len(shape)
    return pl.BlockSpec(shape, lambda b, s: (0,) * nd, pipeline_mode=pl.Buffered(1))


def _tile_spec(tile, width):
    return pl.BlockSpec((None, tile, width), lambda b, s: (b, s, 0))


def _attn_kernel(x_ref, wqkv_ref, bqkv_ref, sinks_ref, bias_ref, wo_ref, g_ref, b_ref, o_ref,
                 q_s, k_s, v_s, att_s):
    j = pl.program_id(1)
    tq = x_ref.shape[0]
    nblk = tq // BLOCK
    hq = ATTN_HEADS * ATTN_HEAD_DIM
    hkv = ATTN_KV_HEADS * ATTN_HEAD_DIM

    @pl.when(j == 0)
    def _():
        k_s[:, 0:BLOCK, :] = jnp.zeros((4, BLOCK, LANES), BF16)
        v_s[:, 0:BLOCK, :] = jnp.zeros((4, BLOCK, LANES), BF16)

    x = x_ref[...]
    qkv = jnp.dot(x.astype(BF16), wqkv_ref[...], preferred_element_type=F32) + bqkv_ref[...]
    q_s[...] = (qkv[:, :hq] * (ATTN_HEAD_DIM ** -0.5)).astype(BF16)
    lo = lax.broadcasted_iota(jnp.int32, (tq, LANES), 1) < ATTN_HEAD_DIM
    for src, dst in ((qkv[:, hq:hq + hkv], k_s), (qkv[:, hq + hkv:hq + 2 * hkv], v_s)):
        rolled = pltpu.roll(src, ATTN_HEAD_DIM, axis=1)
        dst[0, BLOCK:, :] = jnp.where(lo, src, 0.0).astype(BF16)
        dst[1, BLOCK:, :] = jnp.where(lo, 0.0, rolled).astype(BF16)
        dst[2, BLOCK:, :] = jnp.where(lo, rolled, 0.0).astype(BF16)
        dst[3, BLOCK:, :] = jnp.where(lo, 0.0, src).astype(BF16)

    upper = (lax.broadcasted_iota(jnp.int32, (BLOCK, BLOCK), 1)
             > lax.broadcasted_iota(jnp.int32, (BLOCK, BLOCK), 0))
    ones_b = jnp.ones((2 * BLOCK, LANES), BF16)

    def blk_body(bi, carry):
        r0 = pl.multiple_of(bi * BLOCK, BLOCK)
        first = jnp.logical_and(j == 0, bi == 0).astype(jnp.int32)
        for t in range(ATTN_HEADS // 2):
            kvh = t // (ATTN_HEADS // ATTN_KV_HEADS // 2)
            qt = q_s[pl.ds(r0, BLOCK), t * LANES:(t + 1) * LANES]
            acc = None
            for par in range(2):
                h = 2 * t + par
                kb = k_s[2 * kvh + par, pl.ds(r0, 2 * BLOCK), :]
                vb = v_s[2 * kvh + par, pl.ds(r0, 2 * BLOCK), :]
                s = lax.dot_general(qt, kb, (((1,), (1,)), ((), ())), preferred_element_type=F32)
                logits = jnp.where(upper, s[:, :BLOCK], s[:, BLOCK:]) + bias_ref[first, h]
                sink = sinks_ref[h]
                m = jnp.maximum(jnp.max(logits, axis=-1, keepdims=True), sink)
                pexp = jnp.exp(logits - m)
                p2 = jnp.concatenate([jnp.where(upper, pexp, 0.0), jnp.where(upper, 0.0, pexp)], axis=1)
                o2 = jnp.dot(p2.astype(BF16), jnp.concatenate([vb, ones_b], axis=1),
                             preferred_element_type=F32)
                o = o2[:, :LANES] * (1.0 / (o2[:, LANES:] + jnp.exp(sink - m)))
                acc = o if acc is None else acc + o
            att_s[pl.ds(r0, BLOCK), t * LANES:(t + 1) * LANES] = acc.astype(BF16)
        return carry

    lax.fori_loop(0, nblk, blk_body, 0, unroll=2)
    k_s[:, 0:BLOCK, :] = k_s[:, tq:tq + BLOCK, :]
    v_s[:, 0:BLOCK, :] = v_s[:, tq:tq + BLOCK, :]

    mix = jnp.dot(att_s[...], wo_ref[...], preferred_element_type=F32)
    o_ref[...] = _layer_norm(ALPHA * x + mix, g_ref[...], b_ref[...])


def _attn_bias():
    assert WINDOW == BLOCK
    qi = np.arange(BLOCK)[:, None]
    sj = np.arange(BLOCK)[None, :]
    prev = sj > qi
    dist = np.where(prev, qi + BLOCK - sj, qi - sj).astype(np.float64)
    slopes = 2.0 ** (-8.0 * np.arange(1, ATTN_HEADS + 1) / ATTN_HEADS)
    general = -slopes[:, None, None] * dist[None]
    first = np.where(prev[None], -np.inf, general)
    return jnp.asarray(np.stack([general, first]), dtype=F32)


def _attn_layer(x, w_qkv, b_qkv, sinks, w_o, ln_g, ln_b, tile):
    B, S, D = x.shape
    hq = ATTN_HEADS * ATTN_HEAD_DIM
    qkv_w = w_qkv.shape[1]
    bias = _attn_bias()
    return pl.pallas_call(
        _attn_kernel,
        name="attn_layer",
        grid=(B, S // tile),
        in_specs=[
            _tile_spec(tile, D),
            _const_spec((D, qkv_w)),
            _const_spec((1, qkv_w)),
            pl.BlockSpec(memory_space=pltpu.SMEM),
            _const_spec(bias.shape),
            _const_spec((hq, D)),
            _const_spec((1, D)),
            _const_spec((1, D)),
        ],
        out_specs=_tile_spec(tile, D),
        out_shape=jax.ShapeDtypeStruct((B, S, D), F32),
        scratch_shapes=[
            pltpu.VMEM((tile, hq), BF16),
            pltpu.VMEM((4, BLOCK + tile, LANES), BF16),
            pltpu.VMEM((4, BLOCK + tile, LANES), BF16),
            pltpu.VMEM((tile, hq), BF16),
        ],
        compiler_params=pltpu.CompilerParams(
            dimension_semantics=("arbitrary", "arbitrary"), vmem_limit_bytes=VMEM_LIMIT),
    )(x, w_qkv.astype(BF16), b_qkv.reshape(1, qkv_w), sinks.astype(F32), bias,
      w_o.astype(BF16), ln_g.reshape(1, D), ln_b.reshape(1, D))


def _mlstm_kernel(x_ref, win_ref, wgt_ref, bgt_ref, wout_ref, g_ref, b_ref, o_ref,
                  q_s, k_s, km_s, v_s, sig_s, hg_s, cn_s, m_s):
    j = pl.program_id(1)
    ts = x_ref.shape[0]
    L = MLSTM_CHUNK
    H, DK, DV = MLSTM_HEADS, MLSTM_QK_DIM, MLSTM_V_DIM
    nchunk = ts // L
    k_off, v_off, o_off = H * DK, 2 * H * DK, 2 * H * DK + H * DV

    @pl.when(j == 0)
    def _():
        cn_s[...] = jnp.zeros(cn_s.shape, F32)
        m_s[...] = jnp.zeros(m_s.shape, F32)

    x = x_ref[...]
    xb = x.astype(BF16)
    q_s[...] = jnp.dot(xb, win_ref[:, 0:k_off], preferred_element_type=F32).astype(BF16)
    k = jnp.dot(xb, win_ref[:, k_off:v_off], preferred_element_type=F32) * (DK ** -0.5)
    k_s[...] = k
    lo_k = (lax.broadcasted_iota(jnp.int32, k.shape, 1) & (LANES - 1)) < DK
    km_s[0] = jnp.where(lo_k, k, 0.0).astype(BF16)
    km_s[1] = jnp.where(lo_k, 0.0, k).astype(BF16)
    v_s[...] = jnp.dot(xb, win_ref[:, v_off:o_off], preferred_element_type=F32).astype(BF16)
    sig_s[...] = jax.nn.sigmoid(jnp.dot(xb, win_ref[:, o_off:], preferred_element_type=F32))
    gt = lax.dot_general(wgt_ref[...], xb, (((1,), (1,)), ((), ())),
                         preferred_element_type=F32) + bgt_ref[...]

    row = lax.broadcasted_iota(jnp.int32, (L, L), 0)
    col = lax.broadcasted_iota(jnp.int32, (L, L), 1)
    causal = col <= row
    triu = (row <= col).astype(F32)
    lane_in = lax.broadcasted_iota(jnp.int32, (H, ts), 1) & (L - 1)
    ig = gt[0:H]
    lf = jax.nn.log_sigmoid(gt[H:2 * H])
    bcum = jnp.concatenate(
        [jnp.dot(lf[:, c * L:(c + 1) * L], triu, preferred_element_type=F32,
                 precision=lax.Precision.HIGHEST) for c in range(nchunk)], axis=1)
    g = ig - bcum
    cmax = g
    shift = 1
    while shift < L:
        cmax = jnp.maximum(cmax, jnp.where(lane_in >= shift, pltpu.roll(cmax, shift, axis=1), NEG_INF))
        shift *= 2
    m = m_s[:, 0:1]
    a_parts, mprev_parts, alast_parts = [], [], []
    for c in range(nchunk):
        a_c = jnp.maximum(cmax[:, c * L:(c + 1) * L], m)
        a_last = a_c[:, L - 1:L]
        a_parts.append(a_c)
        mprev_parts.append(jnp.broadcast_to(m, (H, L)))
        alast_parts.append(jnp.broadcast_to(a_last, (H, L)))
        m = bcum[:, (c + 1) * L - 1:(c + 1) * L] + a_last
    m_s[...] = jnp.broadcast_to(m, m_s.shape)
    a = jnp.concatenate(a_parts, axis=1)
    mprev = jnp.concatenate(mprev_parts, axis=1)
    alast = jnp.concatenate(alast_parts, axis=1)
    w_inter = jnp.exp(mprev - a)
    n_floor = jnp.exp(-(bcum + a))
    w_state = jnp.exp(g - alast)
    decay = jnp.exp(mprev - alast)
    tok_rows = jnp.concatenate([a, w_inter, n_floor, w_state, jnp.zeros((L - 4 * H, ts), F32)], axis=0)

    lo = col < DK
    ones_b = jnp.ones((L, LANES), BF16)
    top_rows = lax.broadcasted_iota(jnp.int32, (2 * DK, 2 * LANES), 0) < DK

    def bcast(mat, lane):
        return jnp.broadcast_to(mat[:, lane:lane + 1], (L, LANES))

    for c in range(nchunk):
        rows = slice(c * L, (c + 1) * L)
        tok = tok_rows[:, rows].T
        for t in range(H // 2):
            pair = slice(t * LANES, (t + 1) * LANES)
            qt = q_s[rows, pair]
            for par in range(2):
                h = 2 * t + par
                hv = slice(h * DV, (h + 1) * DV)
                wi_b = bcast(tok, H + h)
                g_b = jnp.broadcast_to(g[h:h + 1, rows], (L, L))
                w_intra = jnp.exp(jnp.where(causal, g_b - bcast(tok, h), NEG_INF))
                s = lax.dot_general(qt, km_s[par, rows, pair], (((1,), (1,)), ((), ())),
                                    preferred_element_type=F32)
                p = (s * w_intra).astype(BF16)
                inter = jnp.dot(qt, cn_s[h].astype(BF16), preferred_element_type=F32)
                intra = jnp.dot(p, jnp.concatenate([v_s[rows, hv], ones_b], axis=1),
                                preferred_element_type=F32)
                num = wi_b * inter[:, :DV] + intra[:, :DV]
                den = wi_b * inter[:, DV:] + intra[:, DV:]
                hh = num / jnp.maximum(jnp.abs(den), bcast(tok, 2 * H + h))
                hg_s[rows, hv] = (sig_s[rows, hv] * hh).astype(BF16)
            kw = k_s[rows, pair] * jnp.where(lo, bcast(tok, 3 * H + 2 * t), bcast(tok, 3 * H + 2 * t + 1))
            upd = jnp.dot(kw.T.astype(BF16),
                          jnp.concatenate([v_s[rows, 2 * t * DV:(2 * t + 2) * DV], ones_b], axis=1),
                          preferred_element_type=F32)
            for par in range(2):
                h = 2 * t + par
                d_b = jnp.broadcast_to(decay[h:h + 1, rows], (2 * DK, LANES))
                new = jnp.concatenate([upd[:, par * DV:(par + 1) * DV], upd[:, 2 * DV:]], axis=1)
                keep = top_rows if par == 0 else jnp.logical_not(top_rows)
                cn_s[h] = jnp.concatenate([d_b, d_b], axis=1) * cn_s[h] + jnp.where(keep, new, 0.0)

    mix = jnp.dot(hg_s[...], wout_ref[...], preferred_element_type=F32)
    o_ref[...] = _layer_norm(ALPHA * x + mix, g_ref[...], b_ref[...])


def _mlstm_layer(x, w_in, b_gates, w_out, ln_g, ln_b, tile):
    B, S, D = x.shape
    H, DK, DV = MLSTM_HEADS, MLSTM_QK_DIM, MLSTM_V_DIM
    main_w = 2 * H * DK + 2 * H * DV
    w_main = w_in[:, :main_w].astype(BF16)
    w_gt = w_in[:, main_w:].T.astype(BF16)
    b_gt = jnp.broadcast_to(b_gates.astype(F32)[:, None], (2 * H, tile))
    return pl.pallas_call(
        _mlstm_kernel,
        name="mlstm_layer",
        grid=(B, S // tile),
        in_specs=[
            _tile_spec(tile, D),
            _const_spec((D, main_w)),
            _const_spec((2 * H, D)),
            _const_spec((2 * H, tile)),
            _const_spec((H * DV, D)),
            _const_spec((1, D)),
            _const_spec((1, D)),
        ],
        out_specs=_tile_spec(tile, D),
        out_shape=jax.ShapeDtypeStruct((B, S, D), F32),
        scratch_shapes=[
            pltpu.VMEM((tile, H * DK), BF16),
            pltpu.VMEM((tile, H * DK), F32),
            pltpu.VMEM((2, tile, H * DK), BF16),
            pltpu.VMEM((tile, H * DV), BF16),
            pltpu.VMEM((tile, H * DV), F32),
            pltpu.VMEM((tile, H * DV), BF16),
            pltpu.VMEM((H, 2 * DK, 2 * LANES), F32),
            pltpu.VMEM((H, LANES), F32),
        ],
        compiler_params=pltpu.CompilerParams(
            dimension_semantics=("arbitrary", "arbitrary"), vmem_limit_bytes=VMEM_LIMIT),
    )(x, w_main, w_gt, b_gt, w_out.astype(BF16), ln_g.reshape(1, D), ln_b.reshape(1, D))


def _ffn_kernel(x_ref, p_ref, wup_ref, cw_ref, wdown_ref, wple_ref, wgate_ref, bgate_ref,
                g_ref, b_ref, o_ref, h_s, halo_s, a_s):
    j = pl.program_id(1)
    tm = x_ref.shape[0]
    fc = FF_CHUNK

    @pl.when(j == 0)
    def _():
        halo_s[...] = jnp.zeros(halo_s.shape, F32)

    x = x_ref[...]
    xb = x.astype(BF16)

    def conv_chunk(slot, col0):
        cols = slice(col0, col0 + fc)
        h = jnp.dot(xb, wup_ref[:, cols], preferred_element_type=F32)
        h_s[slot, 0:CONV_HALO, :] = halo_s[:, cols]
        h_s[slot, CONV_HALO:, :] = h
        halo_s[:, cols] = h[tm - CONV_HALO:, :]
        return (cw_ref[0:1, cols] * h_s[slot, CONV_HALO - 2:CONV_HALO - 2 + tm, :]
                + cw_ref[1:2, cols] * h_s[slot, CONV_HALO - 1:CONV_HALO - 1 + tm, :]
                + cw_ref[2:3, cols] * h + cw_ref[3:4, cols])

    for c in range(N_FF_CHUNKS):
        gate = conv_chunk(2 * (c % 2), c * fc)
        value = conv_chunk(2 * (c % 2) + 1, D_FF + c * fc)
        a_s[:, c * fc:(c + 1) * fc] = (jax.nn.gelu(gate, approximate=True) * value).astype(BF16)

    ffn = jnp.dot(a_s[...], wdown_ref[...], preferred_element_type=F32)
    pgate = jax.nn.sigmoid(jnp.dot(xb, wgate_ref[...], preferred_element_type=F32) + bgate_ref[...])
    ple = jnp.dot(p_ref[...].astype(BF16), wple_ref[...], preferred_element_type=F32) * pgate
    o_ref[...] = _layer_norm(ALPHA * x + ffn + ple, g_ref[...], b_ref[...])


def _ffn_layer(x, p, layer, w_up, conv_w, conv_b, w_down, ple_w, gate_w, gate_b, ln_g, ln_b, tile):
    B, S, D = x.shape
    cw = jnp.concatenate([conv_w, conv_b[None], jnp.zeros((8 - CONV_WIDTH - 1, 2 * D_FF), F32)], axis=0)
    return pl.pallas_call(
        _ffn_kernel,
        name="ffn_layer",
        grid=(B, S // tile),
        in_specs=[
            _tile_spec(tile, D),
            pl.BlockSpec((None, None, tile, PLE_DIM), lambda b, s: (layer, b, s, 0)),
            _const_spec((D, 2 * D_FF)),
            _const_spec((8, 2 * D_FF)),
            _const_spec((D_FF, D)),
            _const_spec((PLE_DIM, D)),
            _const_spec((D, D)),
            _const_spec((1, D)),
            _const_spec((1, D)),
            _const_spec((1, D)),
        ],
        out_specs=_tile_spec(tile, D),
        out_shape=jax.ShapeDtypeStruct((B, S, D), F32),
        scratch_shapes=[
            pltpu.VMEM((4, CONV_HALO + tile, FF_CHUNK), F32),
            pltpu.VMEM((CONV_HALO, 2 * D_FF), F32),
            pltpu.VMEM((tile, D_FF), BF16),
        ],
        compiler_params=pltpu.CompilerParams(
            dimension_semantics=("arbitrary", "arbitrary"), vmem_limit_bytes=VMEM_LIMIT),
    )(x, p, w_up.astype(BF16), cw, w_down.astype(BF16), ple_w.astype(BF16), gate_w.astype(BF16),
      gate_b.reshape(1, D), ln_g.reshape(1, D), ln_b.reshape(1, D))


def kernel(x, p, attn_w_qkv, attn_b_qkv, attn_sinks, attn_w_o, mlstm_w_in, mlstm_b_gates, mlstm_w_out, ln1_g, ln1_b, ffn_w_up, ffn_conv_w, ffn_conv_b, ffn_w_down, ple_w, ple_gate_w, ple_gate_b, ln2_g, ln2_b):
    tile = min(SEQ_TILE, x.shape[1])
    for i in range(DEPTH):
        j = i // N_MIXERS
        if i % N_MIXERS == 0:
            x = _attn_layer(x, attn_w_qkv[j], attn_b_qkv[j], attn_sinks[j], attn_w_o[j],
                            ln1_g[i], ln1_b[i], tile)
        else:
            x = _mlstm_layer(x, mlstm_w_in[j], mlstm_b_gates[j], mlstm_w_out[j],
                             ln1_g[i], ln1_b[i], tile)
        x = _ffn_layer(x, p, i, ffn_w_up[i], ffn_conv_w[i], ffn_conv_b[i], ffn_w_down[i],
                       ple_w[i], ple_gate_w[i], ple_gate_b[i], ln2_g[i], ln2_b[i], tile)
    return x
```

```python
import functools

import numpy as np
import jax
import jax.numpy as jnp
from jax import lax
from jax.experimental import pallas as pl
from jax.experimental.pallas import tpu as pltpu

D_MODEL = 1024
DEPTH = 2
N_MIXERS = 2
ATTN_HEADS = 16
ATTN_KV_HEADS = 2
ATTN_HEAD_DIM = 64
WINDOW = 128
BLOCK = 128
MLSTM_HEADS = 8
MLSTM_QK_DIM = 64
MLSTM_V_DIM = 128
D_FF = 2816
CONV_WIDTH = 3
PLE_DIM = 256
LN_EPS = 1e-5
ALPHA = (2.0 * DEPTH) ** 0.25

LANES = 128
MLSTM_CHUNK = 128
FF_CHUNK = 256
N_FF_CHUNKS = D_FF // FF_CHUNK
CONV_HALO = 8
SEQ_TILE = 512
OUT_ROWS = 128
FF_OUT_ROWS = 128
VMEM_LIMIT = 52 * 1024 * 1024

F32 = jnp.float32
BF16 = jnp.bfloat16
NEG_INF = float("-inf")


def _layer_norm(y, g, b):
    mu = jnp.mean(y, axis=-1, keepdims=True)
    d = y - mu
    var = jnp.mean(d * d, axis=-1, keepdims=True)
    return d * lax.rsqrt(var + LN_EPS) * g + b


def _const_spec(shape):
    nd = len(shape)
    return pl.BlockSpec(shape, lambda b, s: (0,) * nd, pipeline_mode=pl.Buffered(1))


def _tile_spec(tile, width):
    return pl.BlockSpec((None, tile, width), lambda b, s: (b, s, 0))


def _attn_kernel(x_ref, wqkv_ref, bqkv_ref, sinks_ref, bias_ref, wo_ref, g_ref, b_ref, o_ref,
                 q_s, k_s, v_s, att_s):
    j = pl.program_id(1)
    tq = x_ref.shape[0]
    nblk = tq // BLOCK
    hq = ATTN_HEADS * ATTN_HEAD_DIM
    hkv = ATTN_KV_HEADS * ATTN_HEAD_DIM

    @pl.when(j == 0)
    def _():
        k_s[:, 0:BLOCK, :] = jnp.zeros((4, BLOCK, LANES), BF16)
        v_s[:, 0:BLOCK, :] = jnp.zeros((4, BLOCK, LANES), BF16)

    x = x_ref[...]
    qkv = jnp.dot(x.astype(BF16), wqkv_ref[...], preferred_element_type=F32) + bqkv_ref[...]
    q_s[...] = (qkv[:, :hq] * (ATTN_HEAD_DIM ** -0.5)).astype(BF16)
    lo = lax.broadcasted_iota(jnp.int32, (tq, LANES), 1) < ATTN_HEAD_DIM
    for src, dst in ((qkv[:, hq:hq + hkv], k_s), (qkv[:, hq + hkv:hq + 2 * hkv], v_s)):
        rolled = pltpu.roll(src, ATTN_HEAD_DIM, axis=1)
        dst[0, BLOCK:, :] = jnp.where(lo, src, 0.0).astype(BF16)
        dst[1, BLOCK:, :] = jnp.where(lo, 0.0, rolled).astype(BF16)
        dst[2, BLOCK:, :] = jnp.where(lo, rolled, 0.0).astype(BF16)
        dst[3, BLOCK:, :] = jnp.where(lo, 0.0, src).astype(BF16)

    upper = (lax.broadcasted_iota(jnp.int32, (BLOCK, BLOCK), 1)
             > lax.broadcasted_iota(jnp.int32, (BLOCK, BLOCK), 0))
    ones_b = jnp.ones((2 * BLOCK, LANES), BF16)

    def blk_body(bi, carry):
        r0 = pl.multiple_of(bi * BLOCK, BLOCK)
        first = jnp.logical_and(j == 0, bi == 0).astype(jnp.int32)
        for t in range(ATTN_HEADS // 2):
            kvh = t // (ATTN_HEADS // ATTN_KV_HEADS // 2)
            qt = q_s[pl.ds(r0, BLOCK), t * LANES:(t + 1) * LANES]
            acc = None
            for par in range(2):
                h = 2 * t + par
                kb = k_s[2 * kvh + par, pl.ds(r0, 2 * BLOCK), :]
                vb = v_s[2 * kvh + par, pl.ds(r0, 2 * BLOCK), :]
                s = lax.dot_general(qt, kb, (((1,), (1,)), ((), ())), preferred_element_type=F32)
                logits = jnp.where(upper, s[:, :BLOCK], s[:, BLOCK:]) + bias_ref[first, h]
                sink = sinks_ref[h]
                m = jnp.maximum(jnp.max(logits, axis=-1, keepdims=True), sink)
                pexp = jnp.exp(logits - m)
                p2 = jnp.concatenate([jnp.where(upper, pexp, 0.0), jnp.where(upper, 0.0, pexp)], axis=1)
                o2 = jnp.dot(p2.astype(BF16), jnp.concatenate([vb, ones_b], axis=1),
                             preferred_element_type=F32)
                o = o2[:, :LANES] * (1.0 / (o2[:, LANES:] + jnp.exp(sink - m)))
                acc = o if acc is None else acc + o
            att_s[pl.ds(r0, BLOCK), t * LANES:(t + 1) * LANES] = acc.astype(BF16)
        return carry

    lax.fori_loop(0, nblk, blk_body, 0, unroll=2)
    k_s[:, 0:BLOCK, :] = k_s[:, tq:tq + BLOCK, :]
    v_s[:, 0:BLOCK, :] = v_s[:, tq:tq + BLOCK, :]

    for r0 in range(0, tq, OUT_ROWS):
        rows = slice(r0, r0 + OUT_ROWS)
        mix = jnp.dot(att_s[rows, :], wo_ref[...], preferred_element_type=F32)
        o_ref[rows, :] = _layer_norm(ALPHA * x_ref[rows, :] + mix, g_ref[...], b_ref[...])


def _attn_bias():
    assert WINDOW == BLOCK
    qi = np.arange(BLOCK)[:, None]
    sj = np.arange(BLOCK)[None, :]
    prev = sj > qi
    dist = np.where(prev, qi + BLOCK - sj, qi - sj).astype(np.float64)
    slopes = 2.0 ** (-8.0 * np.arange(1, ATTN_HEADS + 1) / ATTN_HEADS)
    general = -slopes[:, None, None] * dist[None]
    first = np.where(prev[None], -np.inf, general)
    return jnp.asarray(np.stack([general, first]), dtype=F32)


def _attn_layer(x, w_qkv, b_qkv, sinks, w_o, ln_g, ln_b, tile):
    B, S, D = x.shape
    hq = ATTN_HEADS * ATTN_HEAD_DIM
    qkv_w = w_qkv.shape[1]
    bias = _attn_bias()
    return pl.pallas_call(
        _attn_kernel,
        name="attn_layer",
        grid=(B, S // tile),
        in_specs=[
            _tile_spec(tile, D),
            _const_spec((D, qkv_w)),
            _const_spec((1, qkv_w)),
            pl.BlockSpec(memory_space=pltpu.SMEM),
            _const_spec(bias.shape),
            _const_spec((hq, D)),
            _const_spec((1, D)),
            _const_spec((1, D)),
        ],
        out_specs=_tile_spec(tile, D),
        out_shape=jax.ShapeDtypeStruct((B, S, D), F32),
        scratch_shapes=[
            pltpu.VMEM((tile, hq), BF16),
            pltpu.VMEM((4, BLOCK + tile, LANES), BF16),
            pltpu.VMEM((4, BLOCK + tile, LANES), BF16),
            pltpu.VMEM((tile, hq), BF16),
        ],
        compiler_params=pltpu.CompilerParams(
            dimension_semantics=("arbitrary", "arbitrary"), vmem_limit_bytes=VMEM_LIMIT),
    )(x, w_qkv.astype(BF16), b_qkv.reshape(1, qkv_w), sinks.astype(F32), bias,
      w_o.astype(BF16), ln_g.reshape(1, D), ln_b.reshape(1, D))


def _mlstm_kernel(x_ref, win_ref, wgt_ref, bgt_ref, wout_ref, g_ref, b_ref, o_ref,
                  q_s, k_s, km_s, v_s, sig_s, hg_s, cn_s, m_s):
    j = pl.program_id(1)
    ts = x_ref.shape[0]
    L = MLSTM_CHUNK
    H, DK, DV = MLSTM_HEADS, MLSTM_QK_DIM, MLSTM_V_DIM
    nchunk = ts // L
    k_off, v_off, o_off = H * DK, 2 * H * DK, 2 * H * DK + H * DV

    @pl.when(j == 0)
    def _():
        cn_s[...] = jnp.zeros(cn_s.shape, F32)
        m_s[...] = jnp.zeros(m_s.shape, F32)

    x = x_ref[...]
    xb = x.astype(BF16)
    gt = lax.dot_general(wgt_ref[...], xb, (((1,), (1,)), ((), ())),
                         preferred_element_type=F32) + bgt_ref[...]

    row = lax.broadcasted_iota(jnp.int32, (L, L), 0)
    col = lax.broadcasted_iota(jnp.int32, (L, L), 1)
    causal = col <= row
    triu = (row <= col).astype(F32)
    lane_in = lax.broadcasted_iota(jnp.int32, (H, ts), 1) & (L - 1)
    ig = gt[0:H]
    lf = jax.nn.log_sigmoid(gt[H:2 * H])
    bcum = jnp.concatenate(
        [jnp.dot(lf[:, c * L:(c + 1) * L], triu, preferred_element_type=F32,
                 precision=lax.Precision.HIGHEST) for c in range(nchunk)], axis=1)
    g = ig - bcum
    cmax = g
    shift = 1
    while shift < L:
        cmax = jnp.maximum(cmax, jnp.where(lane_in >= shift, pltpu.roll(cmax, shift, axis=1), NEG_INF))
        shift *= 2
    m = m_s[:, 0:1]
    a_parts, mprev_parts, alast_parts = [], [], []
    for c in range(nchunk):
        a_c = jnp.maximum(cmax[:, c * L:(c + 1) * L], m)
        a_last = a_c[:, L - 1:L]
        a_parts.append(a_c)
        mprev_parts.append(jnp.broadcast_to(m, (H, L)))
        alast_parts.append(jnp.broadcast_to(a_last, (H, L)))
        m = bcum[:, (c + 1) * L - 1:(c + 1) * L] + a_last
    m_s[...] = jnp.broadcast_to(m, m_s.shape)
    a = jnp.concatenate(a_parts, axis=1)
    mprev = jnp.concatenate(mprev_parts, axis=1)
    alast = jnp.concatenate(alast_parts, axis=1)
    n_floor = jnp.exp(-(bcum + a))
    w_state = jnp.exp(g - alast)
    decay = jnp.exp(mprev - alast)
    tok_rows = jnp.concatenate([a, n_floor, jnp.zeros((L - 2 * H, ts), F32)], axis=0)

    q_s[...] = jnp.dot(xb, win_ref[:, 0:k_off], preferred_element_type=F32).astype(BF16)
    k = jnp.dot(xb, win_ref[:, k_off:v_off], preferred_element_type=F32) * (DK ** -0.5)
    k_s[...] = k
    lo_k = (lax.broadcasted_iota(jnp.int32, k.shape, 1) & (LANES - 1)) < DK
    km_s[0] = jnp.where(lo_k, k, 0.0).astype(BF16)
    km_s[1] = jnp.where(lo_k, 0.0, k).astype(BF16)
    v_s[...] = jnp.dot(xb, win_ref[:, v_off:o_off], preferred_element_type=F32).astype(BF16)
    sig_s[...] = jax.nn.sigmoid(jnp.dot(xb, win_ref[:, o_off:], preferred_element_type=F32))

    ones_b = jnp.ones((L, LANES), BF16)
    top_rows = lax.broadcasted_iota(jnp.int32, (2 * DK, 2 * LANES), 0) < DK
    top_k = row < DK

    def bcast(mat, lane):
        return jnp.broadcast_to(mat[:, lane:lane + 1], (L, LANES))

    def row_b(mat, h, rows):
        return jnp.broadcast_to(mat[h:h + 1, rows], (L, L))

    for c in range(nchunk):
        rows = slice(c * L, (c + 1) * L)
        tok = tok_rows[:, rows].T
        for t in range(H // 2):
            pair = slice(t * LANES, (t + 1) * LANES)
            qt = q_s[rows, pair]
            s2 = lax.dot_general(qt, jnp.concatenate([km_s[0, rows, pair], km_s[1, rows, pair]], axis=0),
                                 (((1,), (1,)), ((), ())), preferred_element_type=F32)
            for par in range(2):
                h = 2 * t + par
                hv = slice(h * DV, (h + 1) * DV)
                a_b = bcast(tok, h)
                wi_b = jnp.exp(row_b(mprev, h, rows) - a_b)
                w_intra = jnp.exp(jnp.where(causal, row_b(g, h, rows) - a_b, NEG_INF))
                p = (s2[:, par * L:(par + 1) * L] * w_intra).astype(BF16)
                inter = jnp.dot(qt, cn_s[h].astype(BF16), preferred_element_type=F32)
                intra = jnp.dot(p, jnp.concatenate([v_s[rows, hv], ones_b], axis=1),
                                preferred_element_type=F32)
                num = wi_b * inter[:, :DV] + intra[:, :DV]
                den = wi_b * inter[:, DV:] + intra[:, DV:]
                hh = num / jnp.maximum(jnp.abs(den), bcast(tok, H + h))
                hg_s[rows, hv] = (sig_s[rows, hv] * hh).astype(BF16)
            kw_t = k_s[rows, pair].T * jnp.where(top_k, row_b(w_state, 2 * t, rows),
                                                 row_b(w_state, 2 * t + 1, rows))
            upd = jnp.dot(kw_t.astype(BF16),
                          jnp.concatenate([v_s[rows, 2 * t * DV:(2 * t + 2) * DV], ones_b], axis=1),
                          preferred_element_type=F32)
            for par in range(2):
                h = 2 * t + par
                d_b = jnp.broadcast_to(decay[h:h + 1, rows], (2 * DK, LANES))
                new = jnp.concatenate([upd[:, par * DV:(par + 1) * DV], upd[:, 2 * DV:]], axis=1)
                keep = top_rows if par == 0 else jnp.logical_not(top_rows)
                cn_s[h] = jnp.concatenate([d_b, d_b], axis=1) * cn_s[h] + jnp.where(keep, new, 0.0)

    for r0 in range(0, ts, OUT_ROWS):
        rows = slice(r0, r0 + OUT_ROWS)
        mix = jnp.dot(hg_s[rows, :], wout_ref[...], preferred_element_type=F32)
        o_ref[rows, :] = _layer_norm(ALPHA * x_ref[rows, :] + mix, g_ref[...], b_ref[...])


def _mlstm_layer(x, w_in, b_gates, w_out, ln_g, ln_b, tile):
    B, S, D = x.shape
    H, DK, DV = MLSTM_HEADS, MLSTM_QK_DIM, MLSTM_V_DIM
    main_w = 2 * H * DK + 2 * H * DV
    w_main = w_in[:, :main_w].astype(BF16)
    w_gt = w_in[:, main_w:].T.astype(BF16)
    b_gt = jnp.broadcast_to(b_gates.astype(F32)[:, None], (2 * H, tile))
    return pl.pallas_call(
        _mlstm_kernel,
        name="mlstm_layer",
        grid=(B, S // tile),
        in_specs=[
            _tile_spec(tile, D),
            _const_spec((D, main_w)),
            _const_spec((2 * H, D)),
            _const_spec((2 * H, tile)),
            _const_spec((H * DV, D)),
            _const_spec((1, D)),
            _const_spec((1, D)),
        ],
        out_specs=_tile_spec(tile, D),
        out_shape=jax.ShapeDtypeStruct((B, S, D), F32),
        scratch_shapes=[
            pltpu.VMEM((tile, H * DK), BF16),
            pltpu.VMEM((tile, H * DK), F32),
            pltpu.VMEM((2, tile, H * DK), BF16),
            pltpu.VMEM((tile, H * DV), BF16),
            pltpu.VMEM((tile, H * DV), F32),
            pltpu.VMEM((tile, H * DV), BF16),
            pltpu.VMEM((H, 2 * DK, 2 * LANES), F32),
            pltpu.VMEM((H, LANES), F32),
        ],
        compiler_params=pltpu.CompilerParams(
            dimension_semantics=("arbitrary", "arbitrary"), vmem_limit_bytes=VMEM_LIMIT),
    )(x, w_main, w_gt, b_gt, w_out.astype(BF16), ln_g.reshape(1, D), ln_b.reshape(1, D))


def _ffn_kernel(x_ref, p_ref, wup_ref, cw_ref, wdown_ref, wple_ref, wgate_ref, bgate_ref,
                g_ref, b_ref, o_ref, h_s, halo_s, a_s):
    j = pl.program_id(1)
    tm = x_ref.shape[0]
    fc = FF_CHUNK

    @pl.when(j == 0)
    def _():
        halo_s[...] = jnp.zeros(halo_s.shape, F32)

    x = x_ref[...]
    xb = x.astype(BF16)

    def conv_chunk(slot, col0):
        cols = slice(col0, col0 + fc)
        h = jnp.dot(xb, wup_ref[:, cols], preferred_element_type=F32)
        h_s[slot, 0:CONV_HALO, :] = halo_s[:, cols]
        h_s[slot, CONV_HALO:, :] = h
        halo_s[:, cols] = h[tm - CONV_HALO:, :]
        return (cw_ref[0:1, cols] * h_s[slot, CONV_HALO - 2:CONV_HALO - 2 + tm, :]
                + cw_ref[1:2, cols] * h_s[slot, CONV_HALO - 1:CONV_HALO - 1 + tm, :]
                + cw_ref[2:3, cols] * h + cw_ref[3:4, cols])

    for c in range(N_FF_CHUNKS):
        gate = conv_chunk(2 * (c % 2), c * fc)
        value = conv_chunk(2 * (c % 2) + 1, D_FF + c * fc)
        a_s[:, c * fc:(c + 1) * fc] = (jax.nn.gelu(gate, approximate=True) * value).astype(BF16)

    for r0 in range(0, tm, FF_OUT_ROWS):
        rows = slice(r0, r0 + FF_OUT_ROWS)
        ffn = jnp.dot(a_s[rows, :], wdown_ref[...], preferred_element_type=F32)
        pgate = jax.nn.sigmoid(
            jnp.dot(xb[rows, :], wgate_ref[...], preferred_element_type=F32) + bgate_ref[...])
        ple = jnp.dot(p_ref[rows, :].astype(BF16), wple_ref[...], preferred_element_type=F32) * pgate
        o_ref[rows, :] = _layer_norm(ALPHA * x_ref[rows, :] + ffn + ple, g_ref[...], b_ref[...])


def _ffn_layer(x, p, layer, w_up, conv_w, conv_b, w_down, ple_w, gate_w, gate_b, ln_g, ln_b, tile):
    B, S, D = x.shape
    cw = jnp.concatenate([conv_w, conv_b[None], jnp.zeros((8 - CONV_WIDTH - 1, 2 * D_FF), F32)], axis=0)
    return pl.pallas_call(
        _ffn_kernel,
        name="ffn_layer",
        grid=(B, S // tile),
        in_specs=[
            _tile_spec(tile, D),
            pl.BlockSpec((None, None, tile, PLE_DIM), lambda b, s: (layer, b, s, 0)),
            _const_spec((D, 2 * D_FF)),
            _const_spec((8, 2 * D_FF)),
            _const_spec((D_FF, D)),
            _const_spec((PLE_DIM, D)),
            _const_spec((D, D)),
            _const_spec((1, D)),
            _const_spec((1, D)),
            _const_spec((1, D)),
        ],
        out_specs=_tile_spec(tile, D),
        out_shape=jax.ShapeDtypeStruct((B, S, D), F32),
        scratch_shapes=[
            pltpu.VMEM((4, CONV_HALO + tile, FF_CHUNK), F32),
            pltpu.VMEM((CONV_HALO, 2 * D_FF), F32),
            pltpu.VMEM((tile, D_FF), BF16),
        ],
        compiler_params=pltpu.CompilerParams(
            dimension_semantics=("arbitrary", "arbitrary"), vmem_limit_bytes=VMEM_LIMIT),
    )(x, p, w_up.astype(BF16), cw, w_down.astype(BF16), ple_w.astype(BF16), gate_w.astype(BF16),
      gate_b.reshape(1, D), ln_g.reshape(1, D), ln_b.reshape(1, D))


def kernel(x, p, attn_w_qkv, attn_b_qkv, attn_sinks, attn_w_o, mlstm_w_in, mlstm_b_gates, mlstm_w_out, ln1_g, ln1_b, ffn_w_up, ffn_conv_w, ffn_conv_b, ffn_w_down, ple_w, ple_gate_w, ple_gate_b, ln2_g, ln2_b):
    tile = min(SEQ_TILE, x.shape[1])
    for i in range(DEPTH):
        j = i // N_MIXERS
        if i % N_MIXERS == 0:
            x = _attn_layer(x, attn_w_qkv[j], attn_b_qkv[j], attn_sinks[j], attn_w_o[j],
                            ln1_g[i], ln1_b[i], tile)
        else:
            x = _mlstm_layer(x, mlstm_w_in[j], mlstm_b_gates[j], mlstm_w_out[j],
                             ln1_g[i], ln1_b[i], tile)
        x = _ffn_layer(x, p, i, ffn_w_up[i], ffn_conv_w[i], ffn_conv_b[i], ffn_w_down[i],
                       ple_w[i], ple_gate_w[i], ple_gate_b[i], ln2_g[i], ln2_b[i], tile)
    return x
```

```python
import functools

import numpy as np
import jax
import jax.numpy as jnp
from jax import lax
from jax.experimental import pallas as pl
from jax.experimental.pallas import tpu as pltpu

D_MODEL = 1024
DEPTH = 2
N_MIXERS = 2
ATTN_HEADS = 16
ATTN_KV_HEADS = 2
ATTN_HEAD_DIM = 64
WINDOW = 128
BLOCK = 128
MLSTM_HEADS = 8
MLSTM_QK_DIM = 64
MLSTM_V_DIM = 128
D_FF = 2816
CONV_WIDTH = 3
PLE_DIM = 256
LN_EPS = 1e-5
ALPHA = (2.0 * DEPTH) ** 0.25

LANES = 128
MLSTM_CHUNK = 128
FF_CHUNK = 256
N_FF_CHUNKS = D_FF // FF_CHUNK
CONV_HALO = 8
SEQ_TILE = 512
ATTN_TILE = 1024
MLSTM_TILE = 1024
OUT_ROWS = 128
FF_OUT_ROWS = 128
VMEM_LIMIT = 52 * 1024 * 1024

F32 = jnp.float32
BF16 = jnp.bfloat16
NEG_INF = float("-inf")


def _layer_norm(y, g, b):
    mu = jnp.mean(y, axis=-1, keepdims=True)
    d = y - mu
    var = jnp.mean(d * d, axis=-1, keepdims=True)
    return d * lax.rsqrt(var + LN_EPS) * g + b


def _const_spec(shape):
    nd = len(shape)
    return pl.BlockSpec(shape, lambda b, s: (0,) * nd, pipeline_mode=pl.Buffered(1))


def _layer_spec(shape, layer):
    nd = len(shape)
    return pl.BlockSpec((None,) + tuple(shape), lambda b, s: (layer,) + (0,) * nd,
                        pipeline_mode=pl.Buffered(1))


def _tile_spec(tile, width):
    return pl.BlockSpec((None, tile, width), lambda b, s: (b, s, 0))


def _attn_kernel(x_ref, wqkv_ref, bqkv_ref, sinks_ref, bias_ref, wo_ref, g_ref, b_ref, o_ref,
                 q_s, k_s, v_s, att_s):
    j = pl.program_id(1)
    tq = x_ref.shape[0]
    nblk = tq // BLOCK
    hq = ATTN_HEADS * ATTN_HEAD_DIM
    hkv = ATTN_KV_HEADS * ATTN_HEAD_DIM

    @pl.when(j == 0)
    def _():
        k_s[:, 0:BLOCK, :] = jnp.zeros((4, BLOCK, LANES), BF16)
        v_s[:, 0:BLOCK, :] = jnp.zeros((4, BLOCK, LANES), BF16)

    x = x_ref[...]
    qkv = jnp.dot(x.astype(BF16), wqkv_ref[...], preferred_element_type=F32) + bqkv_ref[...]
    q_s[...] = (qkv[:, :hq] * (ATTN_HEAD_DIM ** -0.5)).astype(BF16)
    lo = lax.broadcasted_iota(jnp.int32, (tq, LANES), 1) < ATTN_HEAD_DIM
    for src, dst in ((qkv[:, hq:hq + hkv], k_s), (qkv[:, hq + hkv:hq + 2 * hkv], v_s)):
        rolled = pltpu.roll(src, ATTN_HEAD_DIM, axis=1)
        dst[0, BLOCK:, :] = jnp.where(lo, src, 0.0).astype(BF16)
        dst[1, BLOCK:, :] = jnp.where(lo, 0.0, rolled).astype(BF16)
        dst[2, BLOCK:, :] = jnp.where(lo, rolled, 0.0).astype(BF16)
        dst[3, BLOCK:, :] = jnp.where(lo, 0.0, src).astype(BF16)

    upper = (lax.broadcasted_iota(jnp.int32, (BLOCK, BLOCK), 1)
             > lax.broadcasted_iota(jnp.int32, (BLOCK, BLOCK), 0))
    ones_b = jnp.ones((2 * BLOCK, LANES), BF16)

    def blk_body(bi, carry):
        r0 = pl.multiple_of(bi * BLOCK, BLOCK)
        first = jnp.logical_and(j == 0, bi == 0).astype(jnp.int32)
        for t in range(ATTN_HEADS // 2):
            kvh = t // (ATTN_HEADS // ATTN_KV_HEADS // 2)
            qt = q_s[pl.ds(r0, BLOCK), t * LANES:(t + 1) * LANES]
            acc = None
            for par in range(2):
                h = 2 * t + par
                kb = k_s[2 * kvh + par, pl.ds(r0, 2 * BLOCK), :]
                vb = v_s[2 * kvh + par, pl.ds(r0, 2 * BLOCK), :]
                s = lax.dot_general(qt, kb, (((1,), (1,)), ((), ())), preferred_element_type=F32)
                logits = jnp.where(upper, s[:, :BLOCK], s[:, BLOCK:]) + bias_ref[first, h]
                sink = sinks_ref[h]
                m = jnp.maximum(jnp.max(logits, axis=-1, keepdims=True), sink)
                pexp = jnp.exp(logits - m)
                p2 = jnp.concatenate([jnp.where(upper, pexp, 0.0), jnp.where(upper, 0.0, pexp)], axis=1)
                o2 = jnp.dot(p2.astype(BF16), jnp.concatenate([vb, ones_b], axis=1),
                             preferred_element_type=F32)
                o = o2[:, :LANES] * (1.0 / (o2[:, LANES:] + jnp.exp(sink - m)))
                acc = o if acc is None else acc + o
            att_s[pl.ds(r0, BLOCK), t * LANES:(t + 1) * LANES] = acc.astype(BF16)
        return carry

    lax.fori_loop(0, nblk, blk_body, 0, unroll=2)
    k_s[:, 0:BLOCK, :] = k_s[:, tq:tq + BLOCK, :]
    v_s[:, 0:BLOCK, :] = v_s[:, tq:tq + BLOCK, :]

    for r0 in range(0, tq, OUT_ROWS):
        rows = slice(r0, r0 + OUT_ROWS)
        mix = jnp.dot(att_s[rows, :], wo_ref[...], preferred_element_type=F32)
        o_ref[rows, :] = _layer_norm(ALPHA * x_ref[rows, :] + mix, g_ref[...], b_ref[...])


def _attn_bias():
    assert WINDOW == BLOCK
    qi = np.arange(BLOCK)[:, None]
    sj = np.arange(BLOCK)[None, :]
    prev = sj > qi
    dist = np.where(prev, qi + BLOCK - sj, qi - sj).astype(np.float64)
    slopes = 2.0 ** (-8.0 * np.arange(1, ATTN_HEADS + 1) / ATTN_HEADS)
    general = -slopes[:, None, None] * dist[None]
    first = np.where(prev[None], -np.inf, general)
    return jnp.asarray(np.stack([general, first]), dtype=F32)


def _attn_layer(x, w_qkv, b_qkv, sinks, w_o, ln_g, ln_b, tile):
    B, S, D = x.shape
    hq = ATTN_HEADS * ATTN_HEAD_DIM
    qkv_w = w_qkv.shape[1]
    bias = _attn_bias()
    return pl.pallas_call(
        _attn_kernel,
        name="attn_layer",
        grid=(B, S // tile),
        in_specs=[
            _tile_spec(tile, D),
            _const_spec((D, qkv_w)),
            _const_spec((1, qkv_w)),
            pl.BlockSpec(memory_space=pltpu.SMEM),
            _const_spec(bias.shape),
            _const_spec((hq, D)),
            _const_spec((1, D)),
            _const_spec((1, D)),
        ],
        out_specs=_tile_spec(tile, D),
        out_shape=jax.ShapeDtypeStruct((B, S, D), F32),
        scratch_shapes=[
            pltpu.VMEM((tile, hq), BF16),
            pltpu.VMEM((4, BLOCK + tile, LANES), BF16),
            pltpu.VMEM((4, BLOCK + tile, LANES), BF16),
            pltpu.VMEM((tile, hq), BF16),
        ],
        compiler_params=pltpu.CompilerParams(
            dimension_semantics=("arbitrary", "arbitrary"), vmem_limit_bytes=VMEM_LIMIT),
    )(x, w_qkv.astype(BF16), b_qkv.reshape(1, qkv_w), sinks.astype(F32), bias,
      w_o.astype(BF16), ln_g.reshape(1, D), ln_b.reshape(1, D))


def _mlstm_kernel(x_ref, win_ref, bgt_ref, wout_ref, g_ref, b_ref, o_ref,
                  q_s, k_s, km_s, v_s, sig_s, hg_s, cn_s, cnb_s, m_s):
    j = pl.program_id(1)
    ts = x_ref.shape[0]
    L = MLSTM_CHUNK
    H, DK, DV = MLSTM_HEADS, MLSTM_QK_DIM, MLSTM_V_DIM
    nchunk = ts // L
    k_off, v_off, o_off = H * DK, 2 * H * DK, 2 * H * DK + H * DV

    @pl.when(j == 0)
    def _():
        cn_s[...] = jnp.zeros(cn_s.shape, F32)
        cnb_s[...] = jnp.zeros(cnb_s.shape, BF16)
        m_s[...] = jnp.zeros(m_s.shape, F32)

    x = x_ref[...]
    xb = x.astype(BF16)
    g_off = o_off + H * DV
    gt = lax.dot_general(win_ref[:, g_off:g_off + 2 * H], xb, (((0,), (1,)), ((), ())),
                         preferred_element_type=F32) + bgt_ref[...]

    row = lax.broadcasted_iota(jnp.int32, (L, L), 0)
    col = lax.broadcasted_iota(jnp.int32, (L, L), 1)
    causal = col <= row
    triu = (row <= col).astype(F32)
    lane_in = lax.broadcasted_iota(jnp.int32, (H, ts), 1) & (L - 1)
    ig = gt[0:H]
    lf = jax.nn.log_sigmoid(gt[H:2 * H])
    bcum = jnp.concatenate(
        [jnp.dot(lf[:, c * L:(c + 1) * L], triu, preferred_element_type=F32,
                 precision=lax.Precision.HIGHEST) for c in range(nchunk)], axis=1)
    g = ig - bcum
    cmax = g
    shift = 1
    while shift < L:
        cmax = jnp.maximum(cmax, jnp.where(lane_in >= shift, pltpu.roll(cmax, shift, axis=1), NEG_INF))
        shift *= 2
    m = m_s[:, 0:1]
    a_parts, mprev_parts, alast_parts = [], [], []
    for c in range(nchunk):
        a_c = jnp.maximum(cmax[:, c * L:(c + 1) * L], m)
        a_last = a_c[:, L - 1:L]
        a_parts.append(a_c)
        mprev_parts.append(jnp.broadcast_to(m, (H, L)))
        alast_parts.append(jnp.broadcast_to(a_last, (H, L)))
        m = bcum[:, (c + 1) * L - 1:(c + 1) * L] + a_last
    m_s[...] = jnp.broadcast_to(m, m_s.shape)
    a = jnp.concatenate(a_parts, axis=1)
    mprev = jnp.concatenate(mprev_parts, axis=1)
    alast = jnp.concatenate(alast_parts, axis=1)
    n_floor = jnp.exp(-(bcum + a))
    w_state = jnp.exp(g - alast)
    decay = jnp.exp(mprev - alast)
    tok_rows = jnp.concatenate([a, n_floor, jnp.zeros((L - 2 * H, ts), F32)], axis=0)

    q_s[...] = jnp.dot(xb, win_ref[:, 0:k_off], preferred_element_type=F32).astype(BF16)
    k = jnp.dot(xb, win_ref[:, k_off:v_off], preferred_element_type=F32) * (DK ** -0.5)
    k_s[...] = k
    lo_k = (lax.broadcasted_iota(jnp.int32, k.shape, 1) & (LANES - 1)) < DK
    km_s[0] = jnp.where(lo_k, k, 0.0).astype(BF16)
    km_s[1] = jnp.where(lo_k, 0.0, k).astype(BF16)
    v_s[...] = jnp.dot(xb, win_ref[:, v_off:o_off], preferred_element_type=F32).astype(BF16)
    sig_s[...] = jax.nn.sigmoid(jnp.dot(xb, win_ref[:, o_off:g_off], preferred_element_type=F32))

    ones_b = jnp.ones((L, LANES), BF16)
    top_k = row < DK

    def bcast(mat, lane):
        return jnp.broadcast_to(mat[:, lane:lane + 1], (L, LANES))

    def row_b(mat, h, rows):
        return jnp.broadcast_to(mat[h:h + 1, rows], (L, L))

    for c in range(nchunk):
        rows = slice(c * L, (c + 1) * L)
        tok = tok_rows[:, rows].T
        for t in range(H // 2):
            pair = slice(t * LANES, (t + 1) * LANES)
            qt = q_s[rows, pair]
            s2 = lax.dot_general(qt, jnp.concatenate([km_s[0, rows, pair], km_s[1, rows, pair]], axis=0),
                                 (((1,), (1,)), ((), ())), preferred_element_type=F32)
            for par in range(2):
                h = 2 * t + par
                hv = slice(h * DV, (h + 1) * DV)
                a_b = bcast(tok, h)
                wi_b = jnp.exp(row_b(mprev, h, rows) - a_b)
                w_intra = jnp.exp(jnp.where(causal, row_b(g, h, rows) - a_b, NEG_INF))
                p = (s2[:, par * L:(par + 1) * L] * w_intra).astype(BF16)
                inter = jnp.dot(qt, cnb_s[h], preferred_element_type=F32)
                intra = jnp.dot(p, jnp.concatenate([v_s[rows, hv], ones_b], axis=1),
                                preferred_element_type=F32)
                num = wi_b * inter[:, :DV] + intra[:, :DV]
                den = wi_b * inter[:, DV:] + intra[:, DV:]
                hh = num / jnp.maximum(jnp.abs(den), bcast(tok, H + h))
                hg_s[rows, hv] = (sig_s[rows, hv] * hh).astype(BF16)
            kw_t = k_s[rows, pair].T * jnp.where(top_k, row_b(w_state, 2 * t, rows),
                                                 row_b(w_state, 2 * t + 1, rows))
            upd = jnp.dot(kw_t.astype(BF16),
                          jnp.concatenate([v_s[rows, 2 * t * DV:(2 * t + 2) * DV], ones_b], axis=1),
                          preferred_element_type=F32)
            d_pair = jnp.concatenate(
                [jnp.broadcast_to(decay[2 * t + par:2 * t + par + 1, rows], (DK, LANES))
                 for par in range(2)], axis=0)
            kv_pair = jnp.concatenate([upd[0:DK, 0:DV], upd[DK:2 * DK, DV:2 * DV]], axis=0)
            state = (jnp.concatenate([d_pair, d_pair], axis=1) * cn_s[t]
                     + jnp.concatenate([kv_pair, upd[:, 2 * DV:]], axis=1))
            cn_s[t] = state
            cnb_s[2 * t, 0:DK, :] = state[0:DK].astype(BF16)
            cnb_s[2 * t + 1, DK:2 * DK, :] = state[DK:2 * DK].astype(BF16)

    for r0 in range(0, ts, OUT_ROWS):
        rows = slice(r0, r0 + OUT_ROWS)
        mix = jnp.dot(hg_s[rows, :], wout_ref[...], preferred_element_type=F32)
        o_ref[rows, :] = _layer_norm(ALPHA * x_ref[rows, :] + mix, g_ref[...], b_ref[...])


def _mlstm_layer(x, w_in, b_gates, w_out, ln_g, ln_b, tile):
    B, S, D = x.shape
    H, DK, DV = MLSTM_HEADS, MLSTM_QK_DIM, MLSTM_V_DIM
    b_gt = jnp.broadcast_to(b_gates.astype(F32)[:, None], (2 * H, tile))
    return pl.pallas_call(
        _mlstm_kernel,
        name="mlstm_layer",
        grid=(B, S // tile),
        in_specs=[
            _tile_spec(tile, D),
            _const_spec(w_in.shape),
            _const_spec((2 * H, tile)),
            _const_spec((H * DV, D)),
            _const_spec((1, D)),
            _const_spec((1, D)),
        ],
        out_specs=_tile_spec(tile, D),
        out_shape=jax.ShapeDtypeStruct((B, S, D), F32),
        scratch_shapes=[
            pltpu.VMEM((tile, H * DK), BF16),
            pltpu.VMEM((tile, H * DK), F32),
            pltpu.VMEM((2, tile, H * DK), BF16),
            pltpu.VMEM((tile, H * DV), BF16),
            pltpu.VMEM((tile, H * DV), F32),
            pltpu.VMEM((tile, H * DV), BF16),
            pltpu.VMEM((H // 2, 2 * DK, 2 * LANES), F32),
            pltpu.VMEM((H, 2 * DK, 2 * LANES), BF16),
            pltpu.VMEM((H, LANES), F32),
        ],
        compiler_params=pltpu.CompilerParams(
            dimension_semantics=("arbitrary", "arbitrary"), vmem_limit_bytes=VMEM_LIMIT),
    )(x, w_in.astype(BF16), b_gt, w_out.astype(BF16), ln_g.reshape(1, D), ln_b.reshape(1, D))


def _ffn_kernel(x_ref, p_ref, wup_ref, cw_ref, wdown_ref, wple_ref, wgate_ref, bgate_ref,
                g_ref, b_ref, o_ref, h_s, halo_s, a_s):
    j = pl.program_id(1)
    tm = x_ref.shape[0]
    fc = FF_CHUNK

    @pl.when(j == 0)
    def _():
        halo_s[...] = jnp.zeros(halo_s.shape, F32)

    x = x_ref[...]
    xb = x.astype(BF16)

    def conv_chunk(slot, col0):
        cols = slice(col0, col0 + fc)
        h = jnp.dot(xb, wup_ref[:, cols], preferred_element_type=F32)
        h_s[slot, 0:CONV_HALO, :] = halo_s[:, cols]
        h_s[slot, CONV_HALO:, :] = h
        halo_s[:, cols] = h[tm - CONV_HALO:, :]
        return (cw_ref[0:1, cols] * h_s[slot, CONV_HALO - 2:CONV_HALO - 2 + tm, :]
                + cw_ref[1:2, cols] * h_s[slot, CONV_HALO - 1:CONV_HALO - 1 + tm, :]
                + cw_ref[2:3, cols] * h + cw_ref[3:4, cols])

    for c in range(N_FF_CHUNKS):
        gate = conv_chunk(2 * (c % 2), c * fc)
        value = conv_chunk(2 * (c % 2) + 1, D_FF + c * fc)
        a_s[:, c * fc:(c + 1) * fc] = (jax.nn.gelu(gate, approximate=True) * value).astype(BF16)

    for r0 in range(0, tm, FF_OUT_ROWS):
        rows = slice(r0, r0 + FF_OUT_ROWS)
        ffn = jnp.dot(a_s[rows, :], wdown_ref[...], preferred_element_type=F32)
        pgate = jax.nn.sigmoid(
            jnp.dot(xb[rows, :], wgate_ref[...], preferred_element_type=F32) + bgate_ref[...])
        ple = jnp.dot(p_ref[rows, :].astype(BF16), wple_ref[...], preferred_element_type=F32) * pgate
        o_ref[rows, :] = _layer_norm(ALPHA * x_ref[rows, :] + ffn + ple, g_ref[...], b_ref[...])


def _ffn_layer(x, p, layer, w_up, cw, w_down, ple_w, gate_w, gate_b, ln_g, ln_b, tile):
    B, S, D = x.shape
    return pl.pallas_call(
        _ffn_kernel,
        name="ffn_layer",
        grid=(B, S // tile),
        in_specs=[
            _tile_spec(tile, D),
            pl.BlockSpec((None, None, tile, PLE_DIM), lambda b, s: (layer, b, s, 0)),
            _layer_spec((D, 2 * D_FF), layer),
            _layer_spec((8, 2 * D_FF), layer),
            _layer_spec((D_FF, D), layer),
            _layer_spec((PLE_DIM, D), layer),
            _layer_spec((D, D), layer),
            _layer_spec((1, D), layer),
            _layer_spec((1, D), layer),
            _layer_spec((1, D), layer),
        ],
        out_specs=_tile_spec(tile, D),
        out_shape=jax.ShapeDtypeStruct((B, S, D), F32),
        scratch_shapes=[
            pltpu.VMEM((4, CONV_HALO + tile, FF_CHUNK), F32),
            pltpu.VMEM((CONV_HALO, 2 * D_FF), F32),
            pltpu.VMEM((tile, D_FF), BF16),
        ],
        compiler_params=pltpu.CompilerParams(
            dimension_semantics=("arbitrary", "arbitrary"), vmem_limit_bytes=VMEM_LIMIT),
    )(x, p, w_up, cw, w_down, ple_w, gate_w, gate_b, ln_g, ln_b)


def kernel(x, p, attn_w_qkv, attn_b_qkv, attn_sinks, attn_w_o, mlstm_w_in, mlstm_b_gates, mlstm_w_out, ln1_g, ln1_b, ffn_w_up, ffn_conv_w, ffn_conv_b, ffn_w_down, ple_w, ple_gate_w, ple_gate_b, ln2_g, ln2_b):
    tile = min(SEQ_TILE, x.shape[1])
    depth, D = ln2_g.shape
    cw = jnp.concatenate([ffn_conv_w, ffn_conv_b[:, None, :],
                          jnp.zeros((depth, 8 - CONV_WIDTH - 1, 2 * D_FF), F32)], axis=1)
    ffn_args = (ffn_w_up.astype(BF16), cw, ffn_w_down.astype(BF16), ple_w.astype(BF16),
                ple_gate_w.astype(BF16), ple_gate_b.reshape(depth, 1, D),
                ln2_g.reshape(depth, 1, D), ln2_b.reshape(depth, 1, D))
    for i in range(DEPTH):
        j = i // N_MIXERS
        if i % N_MIXERS == 0:
            x = _attn_layer(x, attn_w_qkv[j], attn_b_qkv[j], attn_sinks[j], attn_w_o[j],
                            ln1_g[i], ln1_b[i], min(ATTN_TILE, x.shape[1]))
        else:
            x = _mlstm_layer(x, mlstm_w_in[j], mlstm_b_gates[j], mlstm_w_out[j],
                             ln1_g[i], ln1_b[i], min(MLSTM_TILE, x.shape[1]))
        x = _ffn_layer(x, p, i, *ffn_args, tile)
    return x
```

```python
import numpy as np
import jax
import jax.numpy as jnp
from jax import lax
from jax.experimental import pallas as pl
from jax.experimental.pallas import tpu as pltpu

D_MODEL = 1024
DEPTH = 2
N_MIXERS = 2
ATTN_HEADS = 16
ATTN_KV_HEADS = 2
ATTN_HEAD_DIM = 64
WINDOW = 128
BLOCK = 128
MLSTM_HEADS = 8
MLSTM_QK_DIM = 64
MLSTM_V_DIM = 128
D_FF = 2816
CONV_WIDTH = 3
PLE_DIM = 256
LN_EPS = 1e-5
ALPHA = (2.0 * DEPTH) ** 0.25

LANES = 128
MLSTM_CHUNK = 128
FF_CHUNK = 256
N_FF_CHUNKS = D_FF // FF_CHUNK
CONV_HALO = 8
SEQ_TILE = 512
ATTN_TILE = 1024
MLSTM_TILE = 1024
Q_PROJ_COLS = 256
OUT_ROWS = 256
FF_OUT_ROWS = 128
VMEM_LIMIT = 52 * 1024 * 1024

F32 = jnp.float32
BF16 = jnp.bfloat16
NEG_INF = float("-inf")


def _layer_norm(y, g, b):
    mu = jnp.mean(y, axis=-1, keepdims=True)
    d = y - mu
    var = jnp.mean(d * d, axis=-1, keepdims=True)
    return d * lax.rsqrt(var + LN_EPS) * g + b


def _const_spec(shape):
    nd = len(shape)
    return pl.BlockSpec(shape, lambda b, s: (0,) * nd, pipeline_mode=pl.Buffered(1))


def _layer_spec(shape, layer):
    nd = len(shape)
    return pl.BlockSpec((None,) + tuple(shape), lambda b, s: (layer,) + (0,) * nd,
                        pipeline_mode=pl.Buffered(1))


def _tile_spec(tile, width):
    return pl.BlockSpec((None, tile, width), lambda b, s: (b, s, 0))


def _attn_kernel(x_ref, wqkv_ref, bqkv_ref, sinks_ref, bias_ref, wo_ref, g_ref, b_ref, o_ref,
                 q_s, k_s, v_s, att_s):
    j = pl.program_id(1)
    tq = x_ref.shape[0]
    nblk = tq // BLOCK
    hq = ATTN_HEADS * ATTN_HEAD_DIM
    hkv = ATTN_KV_HEADS * ATTN_HEAD_DIM

    @pl.when(j == 0)
    def _():
        k_s[:, 0:BLOCK, :] = jnp.zeros((4, BLOCK, LANES), BF16)
        v_s[:, 0:BLOCK, :] = jnp.zeros((4, BLOCK, LANES), BF16)

    xb = x_ref[...].astype(BF16)

    def proj(c0, c1):
        return (jnp.dot(xb, wqkv_ref[:, c0:c1], preferred_element_type=F32) + bqkv_ref[:, c0:c1])

    kv = proj(hq, hq + 2 * hkv)
    lo = lax.broadcasted_iota(jnp.int32, (tq, LANES), 1) < ATTN_HEAD_DIM
    for src, dst in ((kv[:, :hkv], k_s), (kv[:, hkv:], v_s)):
        rolled = pltpu.roll(src, ATTN_HEAD_DIM, axis=1)
        dst[0, BLOCK:, :] = jnp.where(lo, src, 0.0).astype(BF16)
        dst[1, BLOCK:, :] = jnp.where(lo, 0.0, rolled).astype(BF16)
        dst[2, BLOCK:, :] = jnp.where(lo, rolled, 0.0).astype(BF16)
        dst[3, BLOCK:, :] = jnp.where(lo, 0.0, src).astype(BF16)
    for c0 in range(0, hq, Q_PROJ_COLS):
        q_s[:, c0:c0 + Q_PROJ_COLS] = (
            proj(c0, c0 + Q_PROJ_COLS) * (ATTN_HEAD_DIM ** -0.5)).astype(BF16)

    upper = (lax.broadcasted_iota(jnp.int32, (BLOCK, BLOCK), 1)
             > lax.broadcasted_iota(jnp.int32, (BLOCK, BLOCK), 0))
    ones_b = jnp.ones((2 * BLOCK, LANES), BF16)

    def blk_body(bi, carry):
        r0 = pl.multiple_of(bi * BLOCK, BLOCK)
        first = jnp.logical_and(j == 0, bi == 0).astype(jnp.int32)
        for t in range(ATTN_HEADS // 2):
            kvh = t // (ATTN_HEADS // ATTN_KV_HEADS // 2)
            qt = q_s[pl.ds(r0, BLOCK), t * LANES:(t + 1) * LANES]
            acc = None
            for par in range(2):
                h = 2 * t + par
                kb = k_s[2 * kvh + par, pl.ds(r0, 2 * BLOCK), :]
                vb = v_s[2 * kvh + par, pl.ds(r0, 2 * BLOCK), :]
                s = lax.dot_general(qt, kb, (((1,), (1,)), ((), ())), preferred_element_type=F32)
                logits = jnp.where(upper, s[:, :BLOCK], s[:, BLOCK:]) + bias_ref[first, h]
                sink = sinks_ref[h]
                m = jnp.maximum(jnp.max(logits, axis=-1, keepdims=True), sink)
                pexp = jnp.exp(logits - m)
                p2 = jnp.concatenate([jnp.where(upper, pexp, 0.0), jnp.where(upper, 0.0, pexp)], axis=1)
                o2 = jnp.dot(p2.astype(BF16), jnp.concatenate([vb, ones_b], axis=1),
                             preferred_element_type=F32)
                o = o2[:, :LANES] * (1.0 / (o2[:, LANES:] + jnp.exp(sink - m)))
                acc = o if acc is None else acc + o
            att_s[pl.ds(r0, BLOCK), t * LANES:(t + 1) * LANES] = acc.astype(BF16)
        return carry

    lax.fori_loop(0, nblk, blk_body, 0, unroll=2)
    k_s[:, 0:BLOCK, :] = k_s[:, tq:tq + BLOCK, :]
    v_s[:, 0:BLOCK, :] = v_s[:, tq:tq + BLOCK, :]

    for r0 in range(0, tq, OUT_ROWS):
        rows = slice(r0, r0 + OUT_ROWS)
        mix = jnp.dot(att_s[rows, :], wo_ref[...], preferred_element_type=F32)
        o_ref[rows, :] = _layer_norm(ALPHA * x_ref[rows, :] + mix, g_ref[...], b_ref[...])


def _attn_bias():
    assert WINDOW == BLOCK
    qi = np.arange(BLOCK)[:, None]
    sj = np.arange(BLOCK)[None, :]
    prev = sj > qi
    dist = np.where(prev, qi + BLOCK - sj, qi - sj).astype(np.float64)
    slopes = 2.0 ** (-8.0 * np.arange(1, ATTN_HEADS + 1) / ATTN_HEADS)
    general = -slopes[:, None, None] * dist[None]
    first = np.where(prev[None], -np.inf, general)
    return jnp.asarray(np.stack([general, first]), dtype=F32)


def _attn_layer(x, w_qkv, b_qkv, sinks, w_o, ln_g, ln_b, tile):
    B, S, D = x.shape
    hq = ATTN_HEADS * ATTN_HEAD_DIM
    qkv_w = w_qkv.shape[1]
    bias = _attn_bias()
    return pl.pallas_call(
        _attn_kernel,
        name="attn_layer",
        grid=(B, S // tile),
        in_specs=[
            _tile_spec(tile, D),
            _const_spec((D, qkv_w)),
            _const_spec((1, qkv_w)),
            pl.BlockSpec(memory_space=pltpu.SMEM),
            _const_spec(bias.shape),
            _const_spec((hq, D)),
            _const_spec((1, D)),
            _const_spec((1, D)),
        ],
        out_specs=_tile_spec(tile, D),
        out_shape=jax.ShapeDtypeStruct((B, S, D), F32),
        scratch_shapes=[
            pltpu.VMEM((tile, hq), BF16),
            pltpu.VMEM((4, BLOCK + tile, LANES), BF16),
            pltpu.VMEM((4, BLOCK + tile, LANES), BF16),
            pltpu.VMEM((tile, hq), BF16),
        ],
        compiler_params=pltpu.CompilerParams(
            dimension_semantics=("arbitrary", "arbitrary"), vmem_limit_bytes=VMEM_LIMIT),
    )(x, w_qkv.astype(BF16), b_qkv.reshape(1, qkv_w), sinks.astype(F32), bias,
      w_o.astype(BF16), ln_g.reshape(1, D), ln_b.reshape(1, D))


def _mlstm_kernel(x_ref, win_ref, wgt_ref, bgt_ref, wout_ref, g_ref, b_ref, o_ref,
                  q_s, k_s, km_s, v_s, sig_s, hg_s, cn_s, cnb_s, m_s):
    j = pl.program_id(1)
    ts = x_ref.shape[0]
    L = MLSTM_CHUNK
    H, DK, DV = MLSTM_HEADS, MLSTM_QK_DIM, MLSTM_V_DIM
    nchunk = ts // L
    k_off, v_off, o_off = H * DK, 2 * H * DK, 2 * H * DK + H * DV

    @pl.when(j == 0)
    def _():
        cn_s[...] = jnp.zeros(cn_s.shape, F32)
        cnb_s[...] = jnp.zeros(cnb_s.shape, BF16)
        m_s[...] = jnp.zeros(m_s.shape, F32)

    xb = x_ref[...].astype(BF16)
    q_s[...] = jnp.dot(xb, win_ref[:, 0:k_off], preferred_element_type=F32).astype(BF16)
    g_off = o_off + H * DV
    gt = lax.dot_general(wgt_ref[...], xb, (((1,), (1,)), ((), ())),
                         preferred_element_type=F32) + bgt_ref[...]

    row = lax.broadcasted_iota(jnp.int32, (L, L), 0)
    col = lax.broadcasted_iota(jnp.int32, (L, L), 1)
    causal = col <= row
    lane_in = lax.broadcasted_iota(jnp.int32, (H, ts), 1) & (L - 1)
    ig = gt[0:H]
    lf = jax.nn.log_sigmoid(gt[H:2 * H])
    k = jnp.dot(xb, win_ref[:, k_off:v_off], preferred_element_type=F32) * (DK ** -0.5)
    k_s[...] = k
    lo_k = (lax.broadcasted_iota(jnp.int32, k.shape, 1) & (LANES - 1)) < DK
    km_s[0] = jnp.where(lo_k, k, 0.0).astype(BF16)
    km_s[1] = jnp.where(lo_k, 0.0, k).astype(BF16)
    def chunk_scan(v, combine, identity):
        shift = 1
        while shift < L:
            v = combine(v, jnp.where(lane_in >= shift, pltpu.roll(v, shift, axis=1), identity))
            shift *= 2
        return v

    bcum = chunk_scan(lf, jnp.add, 0.0)
    g = ig - bcum
    cmax = chunk_scan(g, jnp.maximum, NEG_INF)
    m = m_s[:, 0:1]
    a_parts, mprev_parts, alast_parts = [], [], []
    for c in range(nchunk):
        a_c = jnp.maximum(cmax[:, c * L:(c + 1) * L], m)
        a_last = a_c[:, L - 1:L]
        a_parts.append(a_c)
        mprev_parts.append(jnp.broadcast_to(m, (H, L)))
        alast_parts.append(jnp.broadcast_to(a_last, (H, L)))
        m = bcum[:, (c + 1) * L - 1:(c + 1) * L] + a_last
    m_s[...] = jnp.broadcast_to(m, m_s.shape)
    a = jnp.concatenate(a_parts, axis=1)
    mprev = jnp.concatenate(mprev_parts, axis=1)
    alast = jnp.concatenate(alast_parts, axis=1)
    n_floor = jnp.exp(-(bcum + a))
    w_state = jnp.exp(g - alast)
    decay = jnp.exp(mprev - alast)
    tok_rows = jnp.concatenate([a, n_floor, jnp.zeros((L - 2 * H, ts), F32)], axis=0)

    v_s[...] = jnp.dot(xb, win_ref[:, v_off:o_off], preferred_element_type=F32).astype(BF16)
    sig_s[...] = jax.nn.sigmoid(jnp.dot(xb, win_ref[:, o_off:g_off], preferred_element_type=F32))

    ones_b = jnp.ones((L, LANES), BF16)
    top_k = row < DK

    def bcast(mat, lane):
        return jnp.broadcast_to(mat[:, lane:lane + 1], (L, LANES))

    def row_b(mat, h, rows):
        return jnp.broadcast_to(mat[h:h + 1, rows], (L, L))

    for c in range(nchunk):
        rows = slice(c * L, (c + 1) * L)
        tok = tok_rows[:, rows].T
        for t in range(H // 2):
            pair = slice(t * LANES, (t + 1) * LANES)
            qt = q_s[rows, pair]
            s2 = lax.dot_general(qt, jnp.concatenate([km_s[0, rows, pair], km_s[1, rows, pair]], axis=0),
                                 (((1,), (1,)), ((), ())), preferred_element_type=F32)
            for par in range(2):
                h = 2 * t + par
                hv = slice(h * DV, (h + 1) * DV)
                a_b = bcast(tok, h)
                wi_b = jnp.exp(row_b(mprev, h, rows) - a_b)
                w_intra = jnp.exp(jnp.where(causal, row_b(g, h, rows) - a_b, NEG_INF))
                p = (s2[:, par * L:(par + 1) * L] * w_intra).astype(BF16)
                inter = jnp.dot(qt, cnb_s[h], preferred_element_type=F32)
                intra = jnp.dot(p, jnp.concatenate([v_s[rows, hv], ones_b], axis=1),
                                preferred_element_type=F32)
                num = wi_b * inter[:, :DV] + intra[:, :DV]
                den = wi_b * inter[:, DV:] + intra[:, DV:]
                hh = num / jnp.maximum(jnp.abs(den), bcast(tok, H + h))
                hg_s[rows, hv] = (sig_s[rows, hv] * hh).astype(BF16)
            kw_t = k_s[rows, pair].T * jnp.where(top_k, row_b(w_state, 2 * t, rows),
                                                 row_b(w_state, 2 * t + 1, rows))
            upd = jnp.dot(kw_t.astype(BF16),
                          jnp.concatenate([v_s[rows, 2 * t * DV:(2 * t + 2) * DV], ones_b], axis=1),
                          preferred_element_type=F32)
            d_pair = jnp.concatenate(
                [jnp.broadcast_to(decay[2 * t + par:2 * t + par + 1, rows], (DK, LANES))
                 for par in range(2)], axis=0)
            kv_pair = jnp.concatenate([upd[0:DK, 0:DV], upd[DK:2 * DK, DV:2 * DV]], axis=0)
            state = (jnp.concatenate([d_pair, d_pair], axis=1) * cn_s[t]
                     + jnp.concatenate([kv_pair, upd[:, 2 * DV:]], axis=1))
            cn_s[t] = state
            cnb_s[2 * t, 0:DK, :] = state[0:DK].astype(BF16)
            cnb_s[2 * t + 1, DK:2 * DK, :] = state[DK:2 * DK].astype(BF16)

    for r0 in range(0, ts, OUT_ROWS):
        rows = slice(r0, r0 + OUT_ROWS)
        mix = jnp.dot(hg_s[rows, :], wout_ref[...], preferred_element_type=F32)
        o_ref[rows, :] = _layer_norm(ALPHA * x_ref[rows, :] + mix, g_ref[...], b_ref[...])


def _mlstm_layer(x, w_in, b_gates, w_out, ln_g, ln_b, tile):
    B, S, D = x.shape
    H, DK, DV = MLSTM_HEADS, MLSTM_QK_DIM, MLSTM_V_DIM
    b_gt = jnp.broadcast_to(b_gates.astype(F32)[:, None], (2 * H, tile))
    w_in = w_in.astype(BF16)
    w_gt = w_in[:, w_in.shape[1] - 2 * H:].T
    return pl.pallas_call(
        _mlstm_kernel,
        name="mlstm_layer",
        grid=(B, S // tile),
        in_specs=[
            _tile_spec(tile, D),
            _const_spec(w_in.shape),
            _const_spec((2 * H, D)),
            _const_spec((2 * H, tile)),
            _const_spec((H * DV, D)),
            _const_spec((1, D)),
            _const_spec((1, D)),
        ],
        out_specs=_tile_spec(tile, D),
        out_shape=jax.ShapeDtypeStruct((B, S, D), F32),
        scratch_shapes=[
            pltpu.VMEM((tile, H * DK), BF16),
            pltpu.VMEM((tile, H * DK), F32),
            pltpu.VMEM((2, tile, H * DK), BF16),
            pltpu.VMEM((tile, H * DV), BF16),
            pltpu.VMEM((tile, H * DV), F32),
            pltpu.VMEM((tile, H * DV), BF16),
            pltpu.VMEM((H // 2, 2 * DK, 2 * LANES), F32),
            pltpu.VMEM((H, 2 * DK, 2 * LANES), BF16),
            pltpu.VMEM((H, LANES), F32),
        ],
        compiler_params=pltpu.CompilerParams(
            dimension_semantics=("arbitrary", "arbitrary"), vmem_limit_bytes=VMEM_LIMIT),
    )(x, w_in, w_gt, b_gt, w_out.astype(BF16), ln_g.reshape(1, D), ln_b.reshape(1, D))


def _ffn_kernel(x_ref, p_ref, wup_ref, cw_ref, wdown_ref, wple_ref, wgate_ref, bgate_ref,
                g_ref, b_ref, o_ref, h_s, halo_s, a_s):
    j = pl.program_id(1)
    tm = x_ref.shape[0]
    fc = FF_CHUNK

    @pl.when(j == 0)
    def _():
        halo_s[...] = jnp.zeros(halo_s.shape, F32)

    x = x_ref[...]
    xb = x.astype(BF16)

    def conv_chunk(slot, col0):
        cols = slice(col0, col0 + fc)
        h = jnp.dot(xb, wup_ref[:, cols], preferred_element_type=F32)
        h_s[slot, 0:CONV_HALO, :] = halo_s[:, cols]
        h_s[slot, CONV_HALO:, :] = h
        halo_s[:, cols] = h[tm - CONV_HALO:, :]
        return (cw_ref[0:1, cols] * h_s[slot, CONV_HALO - 2:CONV_HALO - 2 + tm, :]
                + cw_ref[1:2, cols] * h_s[slot, CONV_HALO - 1:CONV_HALO - 1 + tm, :]
                + cw_ref[2:3, cols] * h + cw_ref[3:4, cols])

    for c in range(N_FF_CHUNKS):
        gate = conv_chunk(2 * (c % 2), c * fc)
        value = conv_chunk(2 * (c % 2) + 1, D_FF + c * fc)
        a_s[:, c * fc:(c + 1) * fc] = (jax.nn.gelu(gate, approximate=True) * value).astype(BF16)

    for r0 in range(0, tm, FF_OUT_ROWS):
        rows = slice(r0, r0 + FF_OUT_ROWS)
        pgate = jax.nn.sigmoid(
            jnp.dot(xb[rows, :], wgate_ref[...], preferred_element_type=F32) + bgate_ref[...])
        ple = jnp.dot(p_ref[rows, :].astype(BF16), wple_ref[...], preferred_element_type=F32) * pgate
        ffn = jnp.dot(a_s[rows, :], wdown_ref[...], preferred_element_type=F32)
        o_ref[rows, :] = _layer_norm(ALPHA * x_ref[rows, :] + ple + ffn, g_ref[...], b_ref[...])


def _ffn_layer(x, p, layer, w_up, cw, w_down, ple_w, gate_w, gate_b, ln_g, ln_b, tile):
    B, S, D = x.shape
    return pl.pallas_call(
        _ffn_kernel,
        name="ffn_layer",
        grid=(B, S // tile),
        in_specs=[
            _tile_spec(tile, D),
            pl.BlockSpec((None, None, tile, PLE_DIM), lambda b, s: (layer, b, s, 0)),
            _layer_spec((D, 2 * D_FF), layer),
            _layer_spec((8, 2 * D_FF), layer),
            _layer_spec((D_FF, D), layer),
            _layer_spec((PLE_DIM, D), layer),
            _layer_spec((D, D), layer),
            _layer_spec((1, D), layer),
            _layer_spec((1, D), layer),
            _layer_spec((1, D), layer),
        ],
        out_specs=_tile_spec(tile, D),
        out_shape=jax.ShapeDtypeStruct((B, S, D), F32),
        scratch_shapes=[
            pltpu.VMEM((4, CONV_HALO + tile, FF_CHUNK), F32),
            pltpu.VMEM((CONV_HALO, 2 * D_FF), F32),
            pltpu.VMEM((tile, D_FF), BF16),
        ],
        compiler_params=pltpu.CompilerParams(
            dimension_semantics=("arbitrary", "arbitrary"), vmem_limit_bytes=VMEM_LIMIT),
    )(x, p, w_up, cw, w_down, ple_w, gate_w, gate_b, ln_g, ln_b)


def kernel(x, p, attn_w_qkv, attn_b_qkv, attn_sinks, attn_w_o, mlstm_w_in, mlstm_b_gates, mlstm_w_out, ln1_g, ln1_b, ffn_w_up, ffn_conv_w, ffn_conv_b, ffn_w_down, ple_w, ple_gate_w, ple_gate_b, ln2_g, ln2_b):
    tile = min(SEQ_TILE, x.shape[1])
    depth, D = ln2_g.shape
    cw = jnp.concatenate([ffn_conv_w, ffn_conv_b[:, None, :],
                          jnp.zeros((depth, 8 - CONV_WIDTH - 1, 2 * D_FF), F32)], axis=1)
    ffn_args = (ffn_w_up.astype(BF16), cw, ffn_w_down.astype(BF16), ple_w.astype(BF16),
                ple_gate_w.astype(BF16), ple_gate_b.reshape(depth, 1, D),
                ln2_g.reshape(depth, 1, D), ln2_b.reshape(depth, 1, D))
    for i in range(DEPTH):
        j = i // N_MIXERS
        if i % N_MIXERS == 0:
            x = _attn_layer(x, attn_w_qkv[j], attn_b_qkv[j], attn_sinks[j], attn_w_o[j],
                            ln1_g[i], ln1_b[i], min(ATTN_TILE, x.shape[1]))
        else:
            x = _mlstm_layer(x, mlstm_w_in[j], mlstm_b_gates[j], mlstm_w_out[j],
                             ln1_g[i], ln1_b[i], min(MLSTM_TILE, x.shape[1]))
        x = _ffn_layer(x, p, i, *ffn_args, tile)
    return x
```

```python
import numpy as np
import jax
import jax.numpy as jnp
from jax import lax
from jax.experimental import pallas as pl
from jax.experimental.pallas import tpu as pltpu

D_MODEL = 1024
DEPTH = 2
N_MIXERS = 2
ATTN_HEADS = 16
ATTN_KV_HEADS = 2
ATTN_HEAD_DIM = 64
WINDOW = 128
BLOCK = 128
MLSTM_HEADS = 8
MLSTM_QK_DIM = 64
MLSTM_V_DIM = 128
D_FF = 2816
CONV_WIDTH = 3
PLE_DIM = 256
LN_EPS = 1e-5
ALPHA = (2.0 * DEPTH) ** 0.25

LANES = 128
MLSTM_CHUNK = 128
FF_CHUNK = 256
N_FF_CHUNKS = D_FF // FF_CHUNK
CONV_HALO = 8
SEQ_TILE = 512
ATTN_TILE = 1024
MLSTM_TILE = 1024
ATTN_BLOCK_UNROLL = 4
Q_PROJ_COLS = 256
OUT_ROWS = 256
FF_OUT_ROWS = 128
VMEM_LIMIT = 52 * 1024 * 1024

F32 = jnp.float32
BF16 = jnp.bfloat16
NEG_INF = float("-inf")


def _layer_norm(y, g, b):
    mu = jnp.mean(y, axis=-1, keepdims=True)
    d = y - mu
    var = jnp.mean(d * d, axis=-1, keepdims=True)
    return d * lax.rsqrt(var + LN_EPS) * g + b


def _const_spec(shape):
    nd = len(shape)
    return pl.BlockSpec(shape, lambda b, s: (0,) * nd, pipeline_mode=pl.Buffered(1))


def _layer_spec(shape, layer):
    nd = len(shape)
    return pl.BlockSpec((None,) + tuple(shape), lambda b, s: (layer,) + (0,) * nd,
                        pipeline_mode=pl.Buffered(1))


def _tile_spec(tile, width):
    return pl.BlockSpec((None, tile, width), lambda b, s: (b, s, 0))


def _attn_kernel(x_ref, wqkv_ref, bqkv_ref, sinks_ref, bias_ref, wo_ref, g_ref, b_ref, o_ref,
                 q_s, k_s, v_s, att_s):
    j = pl.program_id(1)
    tq = x_ref.shape[0]
    nblk = tq // BLOCK
    hq = ATTN_HEADS * ATTN_HEAD_DIM
    hkv = ATTN_KV_HEADS * ATTN_HEAD_DIM

    @pl.when(j == 0)
    def _():
        k_s[:, 0:BLOCK, :] = jnp.zeros((4, BLOCK, LANES), BF16)
        v_s[:, 0:BLOCK, :] = jnp.zeros((4, BLOCK, LANES), BF16)

    xb = x_ref[...].astype(BF16)

    def proj(c0, c1):
        return (jnp.dot(xb, wqkv_ref[:, c0:c1], preferred_element_type=F32) + bqkv_ref[:, c0:c1])

    kv = proj(hq, hq + 2 * hkv)
    lo = lax.broadcasted_iota(jnp.int32, (tq, LANES), 1) < ATTN_HEAD_DIM
    for src, dst in ((kv[:, :hkv], k_s), (kv[:, hkv:], v_s)):
        rolled = pltpu.roll(src, ATTN_HEAD_DIM, axis=1)
        dst[0, BLOCK:, :] = jnp.where(lo, src, 0.0).astype(BF16)
        dst[1, BLOCK:, :] = jnp.where(lo, 0.0, rolled).astype(BF16)
        dst[2, BLOCK:, :] = jnp.where(lo, rolled, 0.0).astype(BF16)
        dst[3, BLOCK:, :] = jnp.where(lo, 0.0, src).astype(BF16)
    for c0 in range(0, hq, Q_PROJ_COLS):
        q_s[:, c0:c0 + Q_PROJ_COLS] = (
            proj(c0, c0 + Q_PROJ_COLS) * (ATTN_HEAD_DIM ** -0.5)).astype(BF16)

    upper = (lax.broadcasted_iota(jnp.int32, (BLOCK, BLOCK), 1)
             > lax.broadcasted_iota(jnp.int32, (BLOCK, BLOCK), 0))
    ones_b = jnp.ones((2 * BLOCK, LANES), BF16)

    def blk_body(bi, carry):
        r0 = pl.multiple_of(bi * BLOCK, BLOCK)
        first = jnp.logical_and(j == 0, bi == 0).astype(jnp.int32)
        for t in range(ATTN_HEADS // 2):
            kvh = t // (ATTN_HEADS // ATTN_KV_HEADS // 2)
            qt = q_s[pl.ds(r0, BLOCK), t * LANES:(t + 1) * LANES]
            acc = None
            for par in range(2):
                h = 2 * t + par
                kb = k_s[2 * kvh + par, pl.ds(r0, 2 * BLOCK), :]
                vb = v_s[2 * kvh + par, pl.ds(r0, 2 * BLOCK), :]
                s = lax.dot_general(qt, kb, (((1,), (1,)), ((), ())), preferred_element_type=F32)
                logits = jnp.where(upper, s[:, :BLOCK], s[:, BLOCK:]) + bias_ref[first, h]
                sink = sinks_ref[h]
                m = jnp.maximum(jnp.max(logits, axis=-1, keepdims=True), sink)
                pexp = jnp.exp(logits - m)
                p2 = jnp.concatenate([jnp.where(upper, pexp, 0.0), jnp.where(upper, 0.0, pexp)], axis=1)
                o2 = jnp.dot(p2.astype(BF16), jnp.concatenate([vb, ones_b], axis=1),
                             preferred_element_type=F32)
                o = o2[:, :LANES] * (1.0 / (o2[:, LANES:] + jnp.exp(sink - m)))
                acc = o if acc is None else acc + o
            att_s[pl.ds(r0, BLOCK), t * LANES:(t + 1) * LANES] = acc.astype(BF16)
        return carry

    lax.fori_loop(0, nblk, blk_body, 0, unroll=min(ATTN_BLOCK_UNROLL, nblk))
    k_s[:, 0:BLOCK, :] = k_s[:, tq:tq + BLOCK, :]
    v_s[:, 0:BLOCK, :] = v_s[:, tq:tq + BLOCK, :]

    for r0 in range(0, tq, OUT_ROWS):
        rows = slice(r0, r0 + OUT_ROWS)
        mix = jnp.dot(att_s[rows, :], wo_ref[...], preferred_element_type=F32)
        o_ref[rows, :] = _layer_norm(ALPHA * x_ref[rows, :] + mix, g_ref[...], b_ref[...])


def _attn_bias():
    assert WINDOW == BLOCK
    qi = np.arange(BLOCK)[:, None]
    sj = np.arange(BLOCK)[None, :]
    prev = sj > qi
    dist = np.where(prev, qi + BLOCK - sj, qi - sj).astype(np.float64)
    slopes = 2.0 ** (-8.0 * np.arange(1, ATTN_HEADS + 1) / ATTN_HEADS)
    general = -slopes[:, None, None] * dist[None]
    first = np.where(prev[None], -np.inf, general)
    return jnp.asarray(np.stack([general, first]), dtype=F32)


def _attn_layer(x, w_qkv, b_qkv, sinks, w_o, ln_g, ln_b, tile):
    B, S, D = x.shape
    hq = ATTN_HEADS * ATTN_HEAD_DIM
    qkv_w = w_qkv.shape[1]
    bias = _attn_bias()
    return pl.pallas_call(
        _attn_kernel,
        name="attn_layer",
        grid=(B, S // tile),
        in_specs=[
            _tile_spec(tile, D),
            _const_spec((D, qkv_w)),
            _const_spec((1, qkv_w)),
            pl.BlockSpec(memory_space=pltpu.SMEM),
            _const_spec(bias.shape),
            _const_spec((hq, D)),
            _const_spec((1, D)),
            _const_spec((1, D)),
        ],
        out_specs=_tile_spec(tile, D),
        out_shape=jax.ShapeDtypeStruct((B, S, D), F32),
        scratch_shapes=[
            pltpu.VMEM((tile, hq), BF16),
            pltpu.VMEM((4, BLOCK + tile, LANES), BF16),
            pltpu.VMEM((4, BLOCK + tile, LANES), BF16),
            pltpu.VMEM((tile, hq), BF16),
        ],
        compiler_params=pltpu.CompilerParams(
            dimension_semantics=("arbitrary", "arbitrary"), vmem_limit_bytes=VMEM_LIMIT),
    )(x, w_qkv.astype(BF16), b_qkv.reshape(1, qkv_w), sinks.astype(F32), bias,
      w_o.astype(BF16), ln_g.reshape(1, D), ln_b.reshape(1, D))


def _mlstm_kernel(x_ref, win_ref, wgt_ref, bgt_ref, wout_ref, g_ref, b_ref, o_ref,
                  q_s, k_s, km_s, v_s, sig_s, hg_s, cn_s, cnb_s, m_s):
    j = pl.program_id(1)
    ts = x_ref.shape[0]
    L = MLSTM_CHUNK
    H, DK, DV = MLSTM_HEADS, MLSTM_QK_DIM, MLSTM_V_DIM
    nchunk = ts // L
    k_off, v_off, o_off = H * DK, 2 * H * DK, 2 * H * DK + H * DV

    @pl.when(j == 0)
    def _():
        cn_s[...] = jnp.zeros(cn_s.shape, F32)
        cnb_s[...] = jnp.zeros(cnb_s.shape, BF16)
        m_s[...] = jnp.zeros(m_s.shape, F32)

    xb = x_ref[...].astype(BF16)
    q_s[...] = jnp.dot(xb, win_ref[:, 0:k_off], preferred_element_type=F32).astype(BF16)
    g_off = o_off + H * DV
    gt = lax.dot_general(wgt_ref[...], xb, (((1,), (1,)), ((), ())),
                         preferred_element_type=F32) + bgt_ref[...]

    row = lax.broadcasted_iota(jnp.int32, (L, L), 0)
    col = lax.broadcasted_iota(jnp.int32, (L, L), 1)
    causal = col <= row
    lane_in = lax.broadcasted_iota(jnp.int32, (H, ts), 1) & (L - 1)
    ig = gt[0:H]
    lf = jax.nn.log_sigmoid(gt[H:2 * H])
    k = jnp.dot(xb, win_ref[:, k_off:v_off], preferred_element_type=F32) * (DK ** -0.5)
    k_s[...] = k
    lo_k = (lax.broadcasted_iota(jnp.int32, k.shape, 1) & (LANES - 1)) < DK
    km_s[0] = jnp.where(lo_k, k, 0.0).astype(BF16)
    km_s[1] = jnp.where(lo_k, 0.0, k).astype(BF16)
    def chunk_scan(v, combine, identity):
        shift = 1
        while shift < L:
            v = combine(v, jnp.where(lane_in >= shift, pltpu.roll(v, shift, axis=1), identity))
            shift *= 2
        return v

    bcum = chunk_scan(lf, jnp.add, 0.0)
    g = ig - bcum
    cmax = chunk_scan(g, jnp.maximum, NEG_INF)
    m = m_s[:, 0:1]
    a_parts, mprev_parts, alast_parts = [], [], []
    for c in range(nchunk):
        a_c = jnp.maximum(cmax[:, c * L:(c + 1) * L], m)
        a_last = a_c[:, L - 1:L]
        a_parts.append(a_c)
        mprev_parts.append(jnp.broadcast_to(m, (H, L)))
        alast_parts.append(jnp.broadcast_to(a_last, (H, L)))
        m = bcum[:, (c + 1) * L - 1:(c + 1) * L] + a_last
    m_s[...] = jnp.broadcast_to(m, m_s.shape)
    a = jnp.concatenate(a_parts, axis=1)
    mprev = jnp.concatenate(mprev_parts, axis=1)
    alast = jnp.concatenate(alast_parts, axis=1)
    n_floor = jnp.exp(-(bcum + a))
    w_state = jnp.exp(g - alast)
    decay = jnp.exp(mprev - alast)
    tok_rows = jnp.concatenate([a, n_floor, jnp.zeros((L - 2 * H, ts), F32)], axis=0)

    v_s[...] = jnp.dot(xb, win_ref[:, v_off:o_off], preferred_element_type=F32).astype(BF16)
    sig_s[...] = jax.nn.sigmoid(jnp.dot(xb, win_ref[:, o_off:g_off], preferred_element_type=F32))

    ones_b = jnp.ones((L, LANES), BF16)
    top_k = row < DK

    def bcast(mat, lane):
        return jnp.broadcast_to(mat[:, lane:lane + 1], (L, LANES))

    def row_b(mat, h, rows):
        return jnp.broadcast_to(mat[h:h + 1, rows], (L, L))

    for c in range(nchunk):
        rows = slice(c * L, (c + 1) * L)
        tok = tok_rows[:, rows].T
        for t in range(H // 2):
            pair = slice(t * LANES, (t + 1) * LANES)
            qt = q_s[rows, pair]
            s2 = lax.dot_general(qt, jnp.concatenate([km_s[0, rows, pair], km_s[1, rows, pair]], axis=0),
                                 (((1,), (1,)), ((), ())), preferred_element_type=F32)
            for par in range(2):
                h = 2 * t + par
                hv = slice(h * DV, (h + 1) * DV)
                a_b = bcast(tok, h)
                wi_b = jnp.exp(row_b(mprev, h, rows) - a_b)
                w_intra = jnp.exp(jnp.where(causal, row_b(g, h, rows) - a_b, NEG_INF))
                p = (s2[:, par * L:(par + 1) * L] * w_intra).astype(BF16)
                inter = jnp.dot(qt, cnb_s[h], preferred_element_type=F32)
                intra = jnp.dot(p, jnp.concatenate([v_s[rows, hv], ones_b], axis=1),
                                preferred_element_type=F32)
                num = wi_b * inter[:, :DV] + intra[:, :DV]
                den = wi_b * inter[:, DV:] + intra[:, DV:]
                hh = num / jnp.maximum(jnp.abs(den), bcast(tok, H + h))
                hg_s[rows, hv] = (sig_s[rows, hv] * hh).astype(BF16)
            kw_t = k_s[rows, pair].T * jnp.where(top_k, row_b(w_state, 2 * t, rows),
                                                 row_b(w_state, 2 * t + 1, rows))
            upd = jnp.dot(kw_t.astype(BF16),
                          jnp.concatenate([v_s[rows, 2 * t * DV:(2 * t + 2) * DV], ones_b], axis=1),
                          preferred_element_type=F32)
            d_pair = jnp.concatenate(
                [jnp.broadcast_to(decay[2 * t + par:2 * t + par + 1, rows], (DK, LANES))
                 for par in range(2)], axis=0)
            kv_pair = jnp.concatenate([upd[0:DK, 0:DV], upd[DK:2 * DK, DV:2 * DV]], axis=0)
            state = (jnp.concatenate([d_pair, d_pair], axis=1) * cn_s[t]
                     + jnp.concatenate([kv_pair, upd[:, 2 * DV:]], axis=1))
            cn_s[t] = state
            cnb_s[2 * t, 0:DK, :] = state[0:DK].astype(BF16)
            cnb_s[2 * t + 1, DK:2 * DK, :] = state[DK:2 * DK].astype(BF16)

    for r0 in range(0, ts, OUT_ROWS):
        rows = slice(r0, r0 + OUT_ROWS)
        mix = jnp.dot(hg_s[rows, :], wout_ref[...], preferred_element_type=F32)
        o_ref[rows, :] = _layer_norm(ALPHA * x_ref[rows, :] + mix, g_ref[...], b_ref[...])


def _mlstm_layer(x, w_in, b_gates, w_out, ln_g, ln_b, tile):
    B, S, D = x.shape
    H, DK, DV = MLSTM_HEADS, MLSTM_QK_DIM, MLSTM_V_DIM
    b_gt = jnp.broadcast_to(b_gates.astype(F32)[:, None], (2 * H, tile))
    w_in = w_in.astype(BF16)
    w_gt = w_in[:, w_in.shape[1] - 2 * H:].T
    return pl.pallas_call(
        _mlstm_kernel,
        name="mlstm_layer",
        grid=(B, S // tile),
        in_specs=[
            _tile_spec(tile, D),
            _const_spec(w_in.shape),
            _const_spec((2 * H, D)),
            _const_spec((2 * H, tile)),
            _const_spec((H * DV, D)),
            _const_spec((1, D)),
            _const_spec((1, D)),
        ],
        out_specs=_tile_spec(tile, D),
        out_shape=jax.ShapeDtypeStruct((B, S, D), F32),
        scratch_shapes=[
            pltpu.VMEM((tile, H * DK), BF16),
            pltpu.VMEM((tile, H * DK), F32),
            pltpu.VMEM((2, tile, H * DK), BF16),
            pltpu.VMEM((tile, H * DV), BF16),
            pltpu.VMEM((tile, H * DV), F32),
            pltpu.VMEM((tile, H * DV), BF16),
            pltpu.VMEM((H // 2, 2 * DK, 2 * LANES), F32),
            pltpu.VMEM((H, 2 * DK, 2 * LANES), BF16),
            pltpu.VMEM((H, LANES), F32),
        ],
        compiler_params=pltpu.CompilerParams(
            dimension_semantics=("arbitrary", "arbitrary"), vmem_limit_bytes=VMEM_LIMIT),
    )(x, w_in, w_gt, b_gt, w_out.astype(BF16), ln_g.reshape(1, D), ln_b.reshape(1, D))


def _ffn_kernel(x_ref, p_ref, wup_ref, cw_ref, wdown_ref, wple_ref, wgate_ref, bgate_ref,
                g_ref, b_ref, o_ref, h_s, halo_s, a_s):
    j = pl.program_id(1)
    tm = x_ref.shape[0]
    fc = FF_CHUNK

    @pl.when(j == 0)
    def _():
        halo_s[...] = jnp.zeros(halo_s.shape, F32)

    x = x_ref[...]
    xb = x.astype(BF16)

    def conv_chunk(slot, col0):
        cols = slice(col0, col0 + fc)
        h = jnp.dot(xb, wup_ref[:, cols], preferred_element_type=F32)
        h_s[slot, 0:CONV_HALO, :] = halo_s[:, cols]
        h_s[slot, CONV_HALO:, :] = h
        halo_s[:, cols] = h[tm - CONV_HALO:, :]
        return (cw_ref[0:1, cols] * h_s[slot, CONV_HALO - 2:CONV_HALO - 2 + tm, :]
                + cw_ref[1:2, cols] * h_s[slot, CONV_HALO - 1:CONV_HALO - 1 + tm, :]
                + cw_ref[2:3, cols] * h + cw_ref[3:4, cols])

    for c in range(N_FF_CHUNKS):
        gate = conv_chunk(2 * (c % 2), c * fc)
        value = conv_chunk(2 * (c % 2) + 1, D_FF + c * fc)
        a_s[:, c * fc:(c + 1) * fc] = (jax.nn.gelu(gate, approximate=True) * value).astype(BF16)

    for r0 in range(0, tm, FF_OUT_ROWS):
        rows = slice(r0, r0 + FF_OUT_ROWS)
        pgate = jax.nn.sigmoid(
            jnp.dot(xb[rows, :], wgate_ref[...], preferred_element_type=F32) + bgate_ref[...])
        ple = jnp.dot(p_ref[rows, :].astype(BF16), wple_ref[...], preferred_element_type=F32) * pgate
        ffn = jnp.dot(a_s[rows, :], wdown_ref[...], preferred_element_type=F32)
        o_ref[rows, :] = _layer_norm(ALPHA * x_ref[rows, :] + ple + ffn, g_ref[...], b_ref[...])


def _ffn_layer(x, p, layer, w_up, cw, w_down, ple_w, gate_w, gate_b, ln_g, ln_b, tile):
    B, S, D = x.shape
    return pl.pallas_call(
        _ffn_kernel,
        name="ffn_layer",
        grid=(B, S // tile),
        in_specs=[
            _tile_spec(tile, D),
            pl.BlockSpec((None, None, tile, PLE_DIM), lambda b, s: (layer, b, s, 0)),
            _layer_spec((D, 2 * D_FF), layer),
            _layer_spec((8, 2 * D_FF), layer),
            _layer_spec((D_FF, D), layer),
            _layer_spec((PLE_DIM, D), layer),
            _layer_spec((D, D), layer),
            _layer_spec((1, D), layer),
            _layer_spec((1, D), layer),
            _layer_spec((1, D), layer),
        ],
        out_specs=_tile_spec(tile, D),
        out_shape=jax.ShapeDtypeStruct((B, S, D), F32),
        scratch_shapes=[
            pltpu.VMEM((4, CONV_HALO + tile, FF_CHUNK), F32),
            pltpu.VMEM((CONV_HALO, 2 * D_FF), F32),
            pltpu.VMEM((tile, D_FF), BF16),
        ],
        compiler_params=pltpu.CompilerParams(
            dimension_semantics=("arbitrary", "arbitrary"), vmem_limit_bytes=VMEM_LIMIT),
    )(x, p, w_up, cw, w_down, ple_w, gate_w, gate_b, ln_g, ln_b)


def kernel(x, p, attn_w_qkv, attn_b_qkv, attn_sinks, attn_w_o, mlstm_w_in, mlstm_b_gates, mlstm_w_out, ln1_g, ln1_b, ffn_w_up, ffn_conv_w, ffn_conv_b, ffn_w_down, ple_w, ple_gate_w, ple_gate_b, ln2_g, ln2_b):
    tile = min(SEQ_TILE, x.shape[1])
    depth, D = ln2_g.shape
    cw = jnp.concatenate([ffn_conv_w, ffn_conv_b[:, None, :],
                          jnp.zeros((depth, 8 - CONV_WIDTH - 1, 2 * D_FF), F32)], axis=1)
    ffn_args = (ffn_w_up.astype(BF16), cw, ffn_w_down.astype(BF16), ple_w.astype(BF16),
                ple_gate_w.astype(BF16), ple_gate_b.reshape(depth, 1, D),
                ln2_g.reshape(depth, 1, D), ln2_b.reshape(depth, 1, D))
    for i in range(DEPTH):
        j = i // N_MIXERS
        if i % N_MIXERS == 0:
            x = _attn_layer(x, attn_w_qkv[j], attn_b_qkv[j], attn_sinks[j], attn_w_o[j],
                            ln1_g[i], ln1_b[i], min(ATTN_TILE, x.shape[1]))
        else:
            x = _mlstm_layer(x, mlstm_w_in[j], mlstm_b_gates[j], mlstm_w_out[j],
                             ln1_g[i], ln1_b[i], min(MLSTM_TILE, x.shape[1]))
        x = _ffn_layer(x, p, i, *ffn_args, tile)
    return x
```

```python
import numpy as np
import jax
import jax.numpy as jnp
from jax import lax
from jax.experimental import pallas as pl
from jax.experimental.pallas import tpu as pltpu

D_MODEL = 1024
DEPTH = 2
N_MIXERS = 2
ATTN_HEADS = 16
ATTN_KV_HEADS = 2
ATTN_HEAD_DIM = 64
WINDOW = 128
BLOCK = 128
MLSTM_HEADS = 8
MLSTM_QK_DIM = 64
MLSTM_V_DIM = 128
D_FF = 2816
CONV_WIDTH = 3
PLE_DIM = 256
LN_EPS = 1e-5
ALPHA = (2.0 * DEPTH) ** 0.25

LANES = 128
MLSTM_CHUNK = 128
FF_CHUNK = 256
N_FF_CHUNKS = D_FF // FF_CHUNK
CONV_HALO = 8
SEQ_TILE = 512
ATTN_TILE = 1024
MLSTM_TILE = 1024
ATTN_BLOCK_UNROLL = 4
Q_PROJ_COLS = 256
OUT_ROWS = 256
FF_CHUNK_UNROLL = 2
FF_OUT_ROWS = 128
VMEM_LIMIT = 52 * 1024 * 1024

F32 = jnp.float32
BF16 = jnp.bfloat16
NEG_INF = float("-inf")


def _layer_norm(y, g, b):
    mu = jnp.mean(y, axis=-1, keepdims=True)
    d = y - mu
    var = jnp.mean(d * d, axis=-1, keepdims=True)
    return d * lax.rsqrt(var + LN_EPS) * g + b


def _const_spec(shape):
    nd = len(shape)
    return pl.BlockSpec(shape, lambda b, s: (0,) * nd, pipeline_mode=pl.Buffered(1))


def _layer_spec(shape, layer):
    nd = len(shape)
    return pl.BlockSpec((None,) + tuple(shape), lambda b, s: (layer,) + (0,) * nd,
                        pipeline_mode=pl.Buffered(1))


def _tile_spec(tile, width):
    return pl.BlockSpec((None, tile, width), lambda b, s: (b, s, 0))


def _attn_kernel(x_ref, wqkv_ref, bqkv_ref, sinks_ref, bias_ref, wo_ref, g_ref, b_ref, o_ref,
                 q_s, k_s, v_s, att_s):
    j = pl.program_id(1)
    tq = x_ref.shape[0]
    nblk = tq // BLOCK
    hq = ATTN_HEADS * ATTN_HEAD_DIM
    hkv = ATTN_KV_HEADS * ATTN_HEAD_DIM

    @pl.when(j == 0)
    def _():
        k_s[:, 0:BLOCK, :] = jnp.zeros((4, BLOCK, LANES), BF16)
        v_s[:, 0:BLOCK, :] = jnp.zeros((4, BLOCK, LANES), BF16)

    xb = x_ref[...].astype(BF16)

    def proj(c0, c1):
        return (jnp.dot(xb, wqkv_ref[:, c0:c1], preferred_element_type=F32) + bqkv_ref[:, c0:c1])

    kv = proj(hq, hq + 2 * hkv)
    lo = lax.broadcasted_iota(jnp.int32, (tq, LANES), 1) < ATTN_HEAD_DIM
    for src, dst in ((kv[:, :hkv], k_s), (kv[:, hkv:], v_s)):
        rolled = pltpu.roll(src, ATTN_HEAD_DIM, axis=1)
        dst[0, BLOCK:, :] = jnp.where(lo, src, 0.0).astype(BF16)
        dst[1, BLOCK:, :] = jnp.where(lo, 0.0, rolled).astype(BF16)
        dst[2, BLOCK:, :] = jnp.where(lo, rolled, 0.0).astype(BF16)
        dst[3, BLOCK:, :] = jnp.where(lo, 0.0, src).astype(BF16)
    for c0 in range(0, hq, Q_PROJ_COLS):
        q_s[:, c0:c0 + Q_PROJ_COLS] = (
            proj(c0, c0 + Q_PROJ_COLS) * (ATTN_HEAD_DIM ** -0.5)).astype(BF16)

    upper = (lax.broadcasted_iota(jnp.int32, (BLOCK, BLOCK), 1)
             > lax.broadcasted_iota(jnp.int32, (BLOCK, BLOCK), 0))
    ones_b = jnp.ones((2 * BLOCK, LANES), BF16)

    def blk_body(bi, carry):
        r0 = pl.multiple_of(bi * BLOCK, BLOCK)
        first = jnp.logical_and(j == 0, bi == 0).astype(jnp.int32)
        for t in range(ATTN_HEADS // 2):
            kvh = t // (ATTN_HEADS // ATTN_KV_HEADS // 2)
            qt = q_s[pl.ds(r0, BLOCK), t * LANES:(t + 1) * LANES]
            acc = None
            for par in range(2):
                h = 2 * t + par
                kb = k_s[2 * kvh + par, pl.ds(r0, 2 * BLOCK), :]
                vb = v_s[2 * kvh + par, pl.ds(r0, 2 * BLOCK), :]
                s = lax.dot_general(qt, kb, (((1,), (1,)), ((), ())), preferred_element_type=F32)
                logits = jnp.where(upper, s[:, :BLOCK], s[:, BLOCK:]) + bias_ref[first, h]
                sink = sinks_ref[h]
                m = jnp.maximum(jnp.max(logits, axis=-1, keepdims=True), sink)
                pexp = jnp.exp(logits - m)
                p2 = jnp.concatenate([jnp.where(upper, pexp, 0.0), jnp.where(upper, 0.0, pexp)], axis=1)
                o2 = jnp.dot(p2.astype(BF16), jnp.concatenate([vb, ones_b], axis=1),
                             preferred_element_type=F32)
                o = o2[:, :LANES] * (1.0 / (o2[:, LANES:] + jnp.exp(sink - m)))
                acc = o if acc is None else acc + o
            att_s[pl.ds(r0, BLOCK), t * LANES:(t + 1) * LANES] = acc.astype(BF16)
        return carry

    lax.fori_loop(0, nblk, blk_body, 0, unroll=min(ATTN_BLOCK_UNROLL, nblk))
    k_s[:, 0:BLOCK, :] = k_s[:, tq:tq + BLOCK, :]
    v_s[:, 0:BLOCK, :] = v_s[:, tq:tq + BLOCK, :]

    for r0 in range(0, tq, OUT_ROWS):
        rows = slice(r0, r0 + OUT_ROWS)
        mix = jnp.dot(att_s[rows, :], wo_ref[...], preferred_element_type=F32)
        o_ref[rows, :] = _layer_norm(ALPHA * x_ref[rows, :] + mix, g_ref[...], b_ref[...])


def _attn_bias():
    assert WINDOW == BLOCK
    qi = np.arange(BLOCK)[:, None]
    sj = np.arange(BLOCK)[None, :]
    prev = sj > qi
    dist = np.where(prev, qi + BLOCK - sj, qi - sj).astype(np.float64)
    slopes = 2.0 ** (-8.0 * np.arange(1, ATTN_HEADS + 1) / ATTN_HEADS)
    general = -slopes[:, None, None] * dist[None]
    first = np.where(prev[None], -np.inf, general)
    return jnp.asarray(np.stack([general, first]), dtype=F32)


def _attn_layer(x, w_qkv, b_qkv, sinks, w_o, ln_g, ln_b, tile):
    B, S, D = x.shape
    hq = ATTN_HEADS * ATTN_HEAD_DIM
    qkv_w = w_qkv.shape[1]
    bias = _attn_bias()
    return pl.pallas_call(
        _attn_kernel,
        name="attn_layer",
        grid=(B, S // tile),
        in_specs=[
            _tile_spec(tile, D),
            _const_spec((D, qkv_w)),
            _const_spec((1, qkv_w)),
            pl.BlockSpec(memory_space=pltpu.SMEM),
            _const_spec(bias.shape),
            _const_spec((hq, D)),
            _const_spec((1, D)),
            _const_spec((1, D)),
        ],
        out_specs=_tile_spec(tile, D),
        out_shape=jax.ShapeDtypeStruct((B, S, D), F32),
        scratch_shapes=[
            pltpu.VMEM((tile, hq), BF16),
            pltpu.VMEM((4, BLOCK + tile, LANES), BF16),
            pltpu.VMEM((4, BLOCK + tile, LANES), BF16),
            pltpu.VMEM((tile, hq), BF16),
        ],
        compiler_params=pltpu.CompilerParams(
            dimension_semantics=("arbitrary", "arbitrary"), vmem_limit_bytes=VMEM_LIMIT),
    )(x, w_qkv.astype(BF16), b_qkv.reshape(1, qkv_w), sinks.astype(F32), bias,
      w_o.astype(BF16), ln_g.reshape(1, D), ln_b.reshape(1, D))


def _mlstm_kernel(x_ref, win_ref, wgt_ref, bgt_ref, wout_ref, g_ref, b_ref, o_ref,
                  q_s, k_s, km_s, v_s, sig_s, hg_s, cn_s, cnb_s, m_s):
    j = pl.program_id(1)
    ts = x_ref.shape[0]
    L = MLSTM_CHUNK
    H, DK, DV = MLSTM_HEADS, MLSTM_QK_DIM, MLSTM_V_DIM
    nchunk = ts // L
    k_off, v_off, o_off = H * DK, 2 * H * DK, 2 * H * DK + H * DV

    @pl.when(j == 0)
    def _():
        cn_s[...] = jnp.zeros(cn_s.shape, F32)
        cnb_s[...] = jnp.zeros(cnb_s.shape, BF16)
        m_s[...] = jnp.zeros(m_s.shape, F32)

    xb = x_ref[...].astype(BF16)
    q_s[...] = jnp.dot(xb, win_ref[:, 0:k_off], preferred_element_type=F32).astype(BF16)
    g_off = o_off + H * DV
    gt = lax.dot_general(wgt_ref[...], xb, (((1,), (1,)), ((), ())),
                         preferred_element_type=F32) + bgt_ref[...]

    row = lax.broadcasted_iota(jnp.int32, (L, L), 0)
    col = lax.broadcasted_iota(jnp.int32, (L, L), 1)
    causal = col <= row
    lane_in = lax.broadcasted_iota(jnp.int32, (H, ts), 1) & (L - 1)
    ig = gt[0:H]
    lf = jax.nn.log_sigmoid(gt[H:2 * H])
    k = jnp.dot(xb, win_ref[:, k_off:v_off], preferred_element_type=F32) * (DK ** -0.5)
    k_s[...] = k
    lo_k = (lax.broadcasted_iota(jnp.int32, k.shape, 1) & (LANES - 1)) < DK
    km_s[0] = jnp.where(lo_k, k, 0.0).astype(BF16)
    km_s[1] = jnp.where(lo_k, 0.0, k).astype(BF16)
    def chunk_scan(v, combine, identity):
        shift = 1
        while shift < L:
            v = combine(v, jnp.where(lane_in >= shift, pltpu.roll(v, shift, axis=1), identity))
            shift *= 2
        return v

    bcum = chunk_scan(lf, jnp.add, 0.0)
    g = ig - bcum
    cmax = chunk_scan(g, jnp.maximum, NEG_INF)
    m = m_s[:, 0:1]
    a_parts, mprev_parts, alast_parts = [], [], []
    for c in range(nchunk):
        a_c = jnp.maximum(cmax[:, c * L:(c + 1) * L], m)
        a_last = a_c[:, L - 1:L]
        a_parts.append(a_c)
        mprev_parts.append(jnp.broadcast_to(m, (H, L)))
        alast_parts.append(jnp.broadcast_to(a_last, (H, L)))
        m = bcum[:, (c + 1) * L - 1:(c + 1) * L] + a_last
    m_s[...] = jnp.broadcast_to(m, m_s.shape)
    a = jnp.concatenate(a_parts, axis=1)
    mprev = jnp.concatenate(mprev_parts, axis=1)
    alast = jnp.concatenate(alast_parts, axis=1)
    n_floor = jnp.exp(-(bcum + a))
    w_state = jnp.exp(g - alast)
    decay = jnp.exp(mprev - alast)
    tok_rows = jnp.concatenate([a, n_floor, jnp.zeros((L - 2 * H, ts), F32)], axis=0)

    v_s[...] = jnp.dot(xb, win_ref[:, v_off:o_off], preferred_element_type=F32).astype(BF16)
    sig_s[...] = jax.nn.sigmoid(jnp.dot(xb, win_ref[:, o_off:g_off], preferred_element_type=F32))

    ones_b = jnp.ones((L, LANES), BF16)
    top_k = row < DK

    def bcast(mat, lane):
        return jnp.broadcast_to(mat[:, lane:lane + 1], (L, LANES))

    def row_b(mat, h, rows):
        return jnp.broadcast_to(mat[h:h + 1, rows], (L, L))

    for c in range(nchunk):
        rows = slice(c * L, (c + 1) * L)
        tok = tok_rows[:, rows].T
        for t in range(H // 2):
            pair = slice(t * LANES, (t + 1) * LANES)
            qt = q_s[rows, pair]
            s2 = lax.dot_general(qt, jnp.concatenate([km_s[0, rows, pair], km_s[1, rows, pair]], axis=0),
                                 (((1,), (1,)), ((), ())), preferred_element_type=F32)
            for par in range(2):
                h = 2 * t + par
                hv = slice(h * DV, (h + 1) * DV)
                a_b = bcast(tok, h)
                wi_b = jnp.exp(row_b(mprev, h, rows) - a_b)
                w_intra = jnp.exp(jnp.where(causal, row_b(g, h, rows) - a_b, NEG_INF))
                p = (s2[:, par * L:(par + 1) * L] * w_intra).astype(BF16)
                inter = jnp.dot(qt, cnb_s[h], preferred_element_type=F32)
                intra = jnp.dot(p, jnp.concatenate([v_s[rows, hv], ones_b], axis=1),
                                preferred_element_type=F32)
                num = wi_b * inter[:, :DV] + intra[:, :DV]
                den = wi_b * inter[:, DV:] + intra[:, DV:]
                hh = num / jnp.maximum(jnp.abs(den), bcast(tok, H + h))
                hg_s[rows, hv] = (sig_s[rows, hv] * hh).astype(BF16)
            kw_t = k_s[rows, pair].T * jnp.where(top_k, row_b(w_state, 2 * t, rows),
                                                 row_b(w_state, 2 * t + 1, rows))
            upd = jnp.dot(kw_t.astype(BF16),
                          jnp.concatenate([v_s[rows, 2 * t * DV:(2 * t + 2) * DV], ones_b], axis=1),
                          preferred_element_type=F32)
            d_pair = jnp.concatenate(
                [jnp.broadcast_to(decay[2 * t + par:2 * t + par + 1, rows], (DK, LANES))
                 for par in range(2)], axis=0)
            kv_pair = jnp.concatenate([upd[0:DK, 0:DV], upd[DK:2 * DK, DV:2 * DV]], axis=0)
            state = (jnp.concatenate([d_pair, d_pair], axis=1) * cn_s[t]
                     + jnp.concatenate([kv_pair, upd[:, 2 * DV:]], axis=1))
            cn_s[t] = state
            cnb_s[2 * t, 0:DK, :] = state[0:DK].astype(BF16)
            cnb_s[2 * t + 1, DK:2 * DK, :] = state[DK:2 * DK].astype(BF16)

    for r0 in range(0, ts, OUT_ROWS):
        rows = slice(r0, r0 + OUT_ROWS)
        mix = jnp.dot(hg_s[rows, :], wout_ref[...], preferred_element_type=F32)
        o_ref[rows, :] = _layer_norm(ALPHA * x_ref[rows, :] + mix, g_ref[...], b_ref[...])


def _mlstm_layer(x, w_in, b_gates, w_out, ln_g, ln_b, tile):
    B, S, D = x.shape
    H, DK, DV = MLSTM_HEADS, MLSTM_QK_DIM, MLSTM_V_DIM
    b_gt = jnp.broadcast_to(b_gates.astype(F32)[:, None], (2 * H, tile))
    w_in = w_in.astype(BF16)
    w_gt = w_in[:, w_in.shape[1] - 2 * H:].T
    return pl.pallas_call(
        _mlstm_kernel,
        name="mlstm_layer",
        grid=(B, S // tile),
        in_specs=[
            _tile_spec(tile, D),
            _const_spec(w_in.shape),
            _const_spec((2 * H, D)),
            _const_spec((2 * H, tile)),
            _const_spec((H * DV, D)),
            _const_spec((1, D)),
            _const_spec((1, D)),
        ],
        out_specs=_tile_spec(tile, D),
        out_shape=jax.ShapeDtypeStruct((B, S, D), F32),
        scratch_shapes=[
            pltpu.VMEM((tile, H * DK), BF16),
            pltpu.VMEM((tile, H * DK), F32),
            pltpu.VMEM((2, tile, H * DK), BF16),
            pltpu.VMEM((tile, H * DV), BF16),
            pltpu.VMEM((tile, H * DV), F32),
            pltpu.VMEM((tile, H * DV), BF16),
            pltpu.VMEM((H // 2, 2 * DK, 2 * LANES), F32),
            pltpu.VMEM((H, 2 * DK, 2 * LANES), BF16),
            pltpu.VMEM((H, LANES), F32),
        ],
        compiler_params=pltpu.CompilerParams(
            dimension_semantics=("arbitrary", "arbitrary"), vmem_limit_bytes=VMEM_LIMIT),
    )(x, w_in, w_gt, b_gt, w_out.astype(BF16), ln_g.reshape(1, D), ln_b.reshape(1, D))


def _ffn_kernel(x_ref, p_ref, wup_ref, cw_ref, wdown_ref, wple_ref, wgate_ref, bgate_ref,
                g_ref, b_ref, o_ref, h_s, halo_s, a_s, xb_s):
    j = pl.program_id(1)
    tm = x_ref.shape[0]
    fc = FF_CHUNK

    @pl.when(j == 0)
    def _():
        halo_s[...] = jnp.zeros(halo_s.shape, F32)

    xb_s[...] = x_ref[...].astype(BF16)

    def chunk_body(c, carry):
        slot = c % 2
        h = jnp.dot(xb_s[...], wup_ref[c], preferred_element_type=F32)
        h_s[slot, 0:CONV_HALO, :] = halo_s[c]
        h_s[slot, CONV_HALO:, :] = h
        halo_s[c] = h[tm - CONV_HALO:, :]
        cw = cw_ref[c]
        conv = (cw[0:1, :] * h_s[slot, CONV_HALO - 2:CONV_HALO - 2 + tm, :]
                + cw[1:2, :] * h_s[slot, CONV_HALO - 1:CONV_HALO - 1 + tm, :]
                + cw[2:3, :] * h + cw[3:4, :])
        a_s[c] = (jax.nn.gelu(conv[:, :fc], approximate=True) * conv[:, fc:]).astype(BF16)
        return carry

    lax.fori_loop(0, N_FF_CHUNKS, chunk_body, 0, unroll=FF_CHUNK_UNROLL)

    for r0 in range(0, tm, FF_OUT_ROWS):
        rows = slice(r0, r0 + FF_OUT_ROWS)
        pgate = jax.nn.sigmoid(
            jnp.dot(xb_s[rows, :], wgate_ref[...], preferred_element_type=F32) + bgate_ref[...])
        ple = jnp.dot(p_ref[rows, :].astype(BF16), wple_ref[...], preferred_element_type=F32) * pgate
        ffn = jnp.dot(a_s[0, rows, :], wdown_ref[0], preferred_element_type=F32)
        for c in range(1, N_FF_CHUNKS):
            ffn += jnp.dot(a_s[c, rows, :], wdown_ref[c], preferred_element_type=F32)
        o_ref[rows, :] = _layer_norm(ALPHA * x_ref[rows, :] + ple + ffn, g_ref[...], b_ref[...])


def _ffn_layer(x, p, layer, w_up, cw, w_down, ple_w, gate_w, gate_b, ln_g, ln_b, tile):
    B, S, D = x.shape
    return pl.pallas_call(
        _ffn_kernel,
        name="ffn_layer",
        grid=(B, S // tile),
        in_specs=[
            _tile_spec(tile, D),
            pl.BlockSpec((None, None, tile, PLE_DIM), lambda b, s: (layer, b, s, 0)),
            _layer_spec((N_FF_CHUNKS, D, 2 * FF_CHUNK), layer),
            _layer_spec((N_FF_CHUNKS, 8, 2 * FF_CHUNK), layer),
            _layer_spec((N_FF_CHUNKS, FF_CHUNK, D), layer),
            _layer_spec((PLE_DIM, D), layer),
            _layer_spec((D, D), layer),
            _layer_spec((1, D), layer),
            _layer_spec((1, D), layer),
            _layer_spec((1, D), layer),
        ],
        out_specs=_tile_spec(tile, D),
        out_shape=jax.ShapeDtypeStruct((B, S, D), F32),
        scratch_shapes=[
            pltpu.VMEM((2, CONV_HALO + tile, 2 * FF_CHUNK), F32),
            pltpu.VMEM((N_FF_CHUNKS, CONV_HALO, 2 * FF_CHUNK), F32),
            pltpu.VMEM((N_FF_CHUNKS, tile, FF_CHUNK), BF16),
            pltpu.VMEM((tile, D), BF16),
        ],
        compiler_params=pltpu.CompilerParams(
            dimension_semantics=("arbitrary", "arbitrary"), vmem_limit_bytes=VMEM_LIMIT),
    )(x, p, w_up, cw, w_down, ple_w, gate_w, gate_b, ln_g, ln_b)


def kernel(x, p, attn_w_qkv, attn_b_qkv, attn_sinks, attn_w_o, mlstm_w_in, mlstm_b_gates, mlstm_w_out, ln1_g, ln1_b, ffn_w_up, ffn_conv_w, ffn_conv_b, ffn_w_down, ple_w, ple_gate_w, ple_gate_b, ln2_g, ln2_b):
    tile = min(SEQ_TILE, x.shape[1])
    depth, D = ln2_g.shape
    cw = jnp.concatenate([ffn_conv_w, ffn_conv_b[:, None, :],
                          jnp.zeros((depth, 8 - CONV_WIDTH - 1, 2 * D_FF), F32)], axis=1)
    nc, fc = N_FF_CHUNKS, FF_CHUNK

    def chunked(w):
        r = w.shape[1]
        return w.reshape(depth, r, 2, nc, fc).transpose(0, 3, 1, 2, 4).reshape(depth, nc, r, 2 * fc)

    ffn_args = (chunked(ffn_w_up.astype(BF16)), chunked(cw),
                ffn_w_down.astype(BF16).reshape(depth, nc, fc, D), ple_w.astype(BF16),
                ple_gate_w.astype(BF16), ple_gate_b.reshape(depth, 1, D),
                ln2_g.reshape(depth, 1, D), ln2_b.reshape(depth, 1, D))
    for i in range(DEPTH):
        j = i // N_MIXERS
        if i % N_MIXERS == 0:
            x = _attn_layer(x, attn_w_qkv[j], attn_b_qkv[j], attn_sinks[j], attn_w_o[j],
                            ln1_g[i], ln1_b[i], min(ATTN_TILE, x.shape[1]))
        else:
            x = _mlstm_layer(x, mlstm_w_in[j], mlstm_b_gates[j], mlstm_w_out[j],
                             ln1_g[i], ln1_b[i], min(MLSTM_TILE, x.shape[1]))
        x = _ffn_layer(x, p, i, *ffn_args, tile)
    return x
```
